```python
import jax, jax.numpy as jnp
from jax import lax
import numpy as np

D_MODEL = 2048
BATCH = 1
SEQ = 8192
DEPTH = 1
DEC_BATCH = 32
DEC_SEQ = 16
PAST_LEN = 4096

CHUNK = 64
N_META = 16
CONV_CH = D_MODEL // 2
POOL_CH = D_MODEL - CONV_CH
MIX_WIDTH = CONV_CH + POOL_CH
CONV_K = 31
POOL_WINDOWS = (2, 4, 8, 16)
N_POOL_GROUPS = len(POOL_WINDOWS)
POOL_GROUP = POOL_CH // N_POOL_GROUPS
POOL_HIST = max(POOL_WINDOWS) - 1
D_FF = ((8 * D_MODEL // 3 + 255) // 256) * 256
EPS = 1e-6

kernel_name = 'hybrid_conv_pool_streaming_encoder_step'


def rmsnorm(x, g):
    xf = x.astype(jnp.float32)
    y = xf * lax.rsqrt(jnp.mean(xf * xf, axis=-1, keepdims=True) + EPS)
    return (y * g.astype(jnp.float32)).astype(x.dtype)


def swiglu(x, w_gate, w_up, w_down):
    return (jax.nn.silu(x @ w_gate) * (x @ w_up)) @ w_down


def causal_dwconv(u_ext, w, b):
    c = u_ext.shape[-1]
    y = lax.conv_general_dilated(
        u_ext, w[:, None, :].astype(u_ext.dtype), window_strides=(1,), padding='VALID',
        dimension_numbers=('NWC', 'WIO', 'NWC'), feature_group_count=c)
    return y + b.astype(y.dtype)


def multiscale_pool(u_ext, n_hist):
    total = u_ext.shape[1]
    uf = u_ext.astype(jnp.float32)
    cs = jnp.concatenate([jnp.zeros_like(uf[:, :1]), jnp.cumsum(uf, axis=1)], axis=1)
    idx = np.arange(n_hist, total)
    cs_end = cs[:, n_hist + 1:]
    outs = []
    for g, w in enumerate(POOL_WINDOWS):
        lo = np.maximum(idx + 1 - w, 0)
        count = jnp.asarray((idx + 1 - lo).astype(np.float32))[None, :, None]
        sl = slice(g * POOL_GROUP, (g + 1) * POOL_GROUP)
        window_sum = cs_end[:, :, sl] - jnp.take(cs[:, :, sl], jnp.asarray(lo), axis=1)
        outs.append(window_sum / count - uf[:, n_hist:, sl])
    return jnp.stack(outs, axis=2).astype(u_ext.dtype)


def encoder_layer(x, conv_hist, pool_hist,
                  ffn1_norm, ffn1_w_gate, ffn1_w_up, ffn1_w_down,
                  mix_norm, w_in, conv_w, conv_b, conv_norm, pool_w, pool_scale, w_out,
                  ffn2_norm, ffn2_w_gate, ffn2_w_up, ffn2_w_down):
    b, l, _ = x.shape
    h = x + 0.5 * swiglu(rmsnorm(x, ffn1_norm), ffn1_w_gate, ffn1_w_up, ffn1_w_down)
    z = rmsnorm(h, mix_norm) @ w_in
    a = z[..., :CONV_CH]
    gate = z[..., CONV_CH:2 * CONV_CH]
    u_pool = z[..., 2 * CONV_CH:]
    u_conv = a * jax.nn.sigmoid(gate)
    conv_ext = jnp.concatenate([conv_hist.astype(u_conv.dtype), u_conv], axis=1)
    c = jax.nn.silu(rmsnorm(causal_dwconv(conv_ext, conv_w, conv_b), conv_norm))
    pool_ext = jnp.concatenate([pool_hist.astype(u_pool.dtype), u_pool], axis=1)
    p = multiscale_pool(pool_ext, pool_hist.shape[1])
    p = jnp.einsum('blgc,gcd->blgd', p, pool_w).reshape(b, l, POOL_CH) * pool_scale
    h = h + jnp.concatenate([c, p], axis=-1) @ w_out
    h = h + 0.5 * swiglu(rmsnorm(h, ffn2_norm), ffn2_w_gate, ffn2_w_up, ffn2_w_down)
    return h, conv_ext[:, -(CONV_K - 1):], pool_ext[:, -POOL_HIST:]


def setup_inputs(seed: int = 0) -> dict:
    key = jax.random.key(seed)
    ks = jax.random.split(key, 24)

    def nrm(k, shape, fan_in):
        return jax.random.normal(k, shape, jnp.float32) * (fan_in ** -0.5)

    def gain(k, shape):
        return 1.0 + 0.05 * jax.random.normal(k, shape, jnp.float32)

    return {
        'x_prompt': jax.random.normal(ks[0], (BATCH, SEQ, D_MODEL), jnp.float32),
        'x_sample': jax.random.normal(ks[1], (DEC_BATCH, DEC_SEQ, D_MODEL), jnp.float32),
        'state_conv': 0.5 * jax.random.normal(ks[2], (DEPTH, DEC_BATCH, CONV_K - 1, CONV_CH), jnp.float32),
        'state_pool': jax.random.normal(ks[3], (DEPTH, DEC_BATCH, POOL_HIST, POOL_CH), jnp.float32),
        'meta_tokens': jax.random.normal(ks[4], (N_META, D_MODEL), jnp.float32),
        'ffn1_norm': gain(ks[5], (DEPTH, D_MODEL)),
        'ffn1_w_gate': nrm(ks[6], (DEPTH, D_MODEL, D_FF), D_MODEL),
        'ffn1_w_up': nrm(ks[7], (DEPTH, D_MODEL, D_FF), D_MODEL),
        'ffn1_w_down': nrm(ks[8], (DEPTH, D_FF, D_MODEL), D_FF),
        'mix_norm': gain(ks[9], (DEPTH, D_MODEL)),
        'w_in': nrm(ks[10], (DEPTH, D_MODEL, 2 * CONV_CH + POOL_CH), D_MODEL),
        'conv_w': nrm(ks[11], (DEPTH, CONV_K, CONV_CH), CONV_K),
        'conv_b': 0.02 * jax.random.normal(ks[12], (DEPTH, CONV_CH), jnp.float32),
        'conv_norm': gain(ks[13], (DEPTH, CONV_CH)),
        'pool_w': nrm(ks[14], (DEPTH, N_POOL_GROUPS, POOL_GROUP, POOL_GROUP), POOL_GROUP),
        'pool_scale': gain(ks[15], (DEPTH, POOL_CH)),
        'w_out': nrm(ks[16], (DEPTH, MIX_WIDTH, D_MODEL), MIX_WIDTH),
        'ffn2_norm': gain(ks[17], (DEPTH, D_MODEL)),
        'ffn2_w_gate': nrm(ks[18], (DEPTH, D_MODEL, D_FF), D_MODEL),
        'ffn2_w_up': nrm(ks[19], (DEPTH, D_MODEL, D_FF), D_MODEL),
        'ffn2_w_down': nrm(ks[20], (DEPTH, D_FF, D_MODEL), D_FF),
        'final_norm': gain(ks[21], (D_MODEL,)),
    }


def reference(x_prompt, x_sample, state_conv, state_pool, meta_tokens,
              ffn1_norm, ffn1_w_gate, ffn1_w_up, ffn1_w_down,
              mix_norm, w_in, conv_w, conv_b, conv_norm, pool_w, pool_scale, w_out,
              ffn2_norm, ffn2_w_gate, ffn2_w_up, ffn2_w_down, final_norm):
    b_p = x_prompt.shape[0]
    meta = jnp.broadcast_to(meta_tokens.astype(x_prompt.dtype)[None], (b_p, N_META, D_MODEL))
    h_p = jnp.concatenate([meta, x_prompt], axis=1)
    h_s = x_sample
    conv_p_list, pool_p_list, conv_s_list, pool_s_list = [], [], [], []
    for d in range(DEPTH):
        params = (ffn1_norm[d], ffn1_w_gate[d], ffn1_w_up[d], ffn1_w_down[d],
                  mix_norm[d], w_in[d], conv_w[d], conv_b[d], conv_norm[d], pool_w[d], pool_scale[d], w_out[d],
                  ffn2_norm[d], ffn2_w_gate[d], ffn2_w_up[d], ffn2_w_down[d])
        conv_pad = jnp.zeros((b_p, CONV_K - 1, CONV_CH), h_p.dtype)
        pool_none = jnp.zeros((b_p, 0, POOL_CH), h_p.dtype)
        h_p, cp, pp = encoder_layer(h_p, conv_pad, pool_none, *params)
        h_s, cs_, ps_ = encoder_layer(h_s, state_conv[d], state_pool[d], *params)
        conv_p_list.append(cp)
        pool_p_list.append(pp)
        conv_s_list.append(cs_)
        pool_s_list.append(ps_)
    y_prompt = rmsnorm(h_p, final_norm)[:, N_META:]
    y_sample = rmsnorm(h_s, final_norm)
    new_conv_prompt = jnp.stack(conv_p_list, axis=0)
    new_pool_prompt = jnp.stack(pool_p_list, axis=0)
    new_conv_sample = jnp.stack(conv_s_list, axis=0)
    new_pool_sample = jnp.stack(pool_s_list, axis=0)
    return (y_prompt, y_sample, new_conv_prompt, new_pool_prompt, new_conv_sample, new_pool_sample)
```

```python
import functools

import jax
import jax.numpy as jnp
from jax import lax
from jax.experimental import pallas as pl
from jax.experimental.pallas import tpu as pltpu

D_MODEL = 2048
N_META = 16
CONV_CH = 1024
POOL_CH = 1024
CONV_K = 31
CONV_HIST = CONV_K - 1
POOL_WINDOWS = (2, 4, 8, 16)
POOL_GROUP = POOL_CH // len(POOL_WINDOWS)
POOL_HIST = max(POOL_WINDOWS) - 1
EPS = 1e-6

SUBLANES = 8
LANES = 128
VMEM_LIMIT_BYTES = 60 * 1024 * 1024

CONV_PAD = 32
POOL_PAD = 16
CONV_OFF = CONV_PAD - CONV_HIST
POOL_OFF = POOL_PAD - POOL_HIST

FFN_ROWS = 1024
FFN_COLS = 512
MIX_ROWS = 512
CONV_ROW_CHUNK = 32


def _rmsnorm(x, g):
    ms = jnp.mean(x * x, axis=-1, keepdims=True)
    return x * lax.rsqrt(ms + EPS) * g


def _dot(a, b):
    return jnp.dot(a, b, preferred_element_type=jnp.float32)


def _ffn_kernel(x_ref, g_ref, wg_ref, wu_ref, wd_ref, fg_ref, o_ref, xn_ref, *, final_norm):
    j = pl.program_id(1)

    @pl.when(j == 0)
    def _():
        x = x_ref[...]
        xn_ref[...] = _rmsnorm(x, g_ref[...]).astype(jnp.bfloat16)
        o_ref[...] = x

    xn = xn_ref[...]
    gate = _dot(xn, wg_ref[...])
    up = _dot(xn, wu_ref[...])
    hid = (0.5 * (gate * jax.nn.sigmoid(gate)) * up).astype(jnp.bfloat16)
    o_ref[...] += _dot(hid, wd_ref[...])

    if final_norm:
        @pl.when(j == pl.num_programs(1) - 1)
        def _():
            o_ref[...] = _rmsnorm(o_ref[...], fg_ref[...])


def _ffn(x, g, wg, wu, wd, fg, *, final_norm, name):
    t, d = x.shape
    f = wg.shape[1]
    tm = FFN_ROWS if t % FFN_ROWS == 0 else t
    tf = FFN_COLS
    assert t % tm == 0 and f % tf == 0
    return pl.pallas_call(
        functools.partial(_ffn_kernel, final_norm=final_norm),
        out_shape=jax.ShapeDtypeStruct((t, d), jnp.float32),
        grid=(t // tm, f // tf),
        in_specs=[
            pl.BlockSpec((tm, d), lambda i, j: (i, 0)),
            pl.BlockSpec((1, d), lambda i, j: (0, 0)),
            pl.BlockSpec((d, tf), lambda i, j: (0, j)),
            pl.BlockSpec((d, tf), lambda i, j: (0, j)),
            pl.BlockSpec((tf, d), lambda i, j: (j, 0)),
            pl.BlockSpec((1, d), lambda i, j: (0, 0)),
        ],
        out_specs=pl.BlockSpec((tm, d), lambda i, j: (i, 0)),
        scratch_shapes=[pltpu.VMEM((tm, d), jnp.bfloat16)],
        compiler_params=pltpu.CompilerParams(
            dimension_semantics=("arbitrary", "arbitrary"),
            vmem_limit_bytes=VMEM_LIMIT_BYTES),
        name=name,
    )(x, g, wg, wu, wd, fg)


def _causal_dwconv(ext_c, cw_ref, cb_ref, y_ref, n_streams, n_rows):
    rc = min(CONV_ROW_CHUNK, n_rows)
    for c0 in range(0, CONV_CH, LANES):
        lanes = slice(c0, c0 + LANES)
        bias = cb_ref[:, lanes]
        for b in range(n_streams):
            for r0 in range(0, n_rows, rc):
                acc = jnp.broadcast_to(bias, (rc, LANES))
                for k in range(CONV_K):
                    lo = CONV_OFF + r0 + k
                    acc = acc + cw_ref[k:k + 1, lanes] * ext_c[b, lo:lo + rc, lanes]
                y_ref[b * n_rows + r0:b * n_rows + r0 + rc, lanes] = acc


def _pool_group(ext_p, grp, window, n_streams, n_rows):
    lanes = slice(grp * POOL_GROUP, (grp + 1) * POOL_GROUP)
    outs = []
    for b in range(n_streams):
        tok = ext_p[b, POOL_PAD:POOL_PAD + n_rows, lanes]
        s = tok
        for i in range(1, window):
            s = s + ext_p[b, POOL_PAD - i:POOL_PAD - i + n_rows, lanes]
        outs.append(s / float(window) - tok)
    return outs[0] if n_streams == 1 else jnp.concatenate(outs, axis=0)


def _mix_tail(h, ext_c, ext_p, y_ref, cw_ref, cb_ref, cn_ref, pw_ref, ps_ref, woc_ref, wop_ref,
              n_streams, n_rows):
    _causal_dwconv(ext_c, cw_ref, cb_ref, y_ref, n_streams, n_rows)
    cn = _rmsnorm(y_ref[...], cn_ref[...])
    c = (cn * jax.nn.sigmoid(cn)).astype(jnp.bfloat16)
    out = h + _dot(c, woc_ref[...])
    for grp, window in enumerate(POOL_WINDOWS):
        lanes = slice(grp * POOL_GROUP, (grp + 1) * POOL_GROUP)
        p = _pool_group(ext_p, grp, window, n_streams, n_rows).astype(jnp.bfloat16)
        p = (_dot(p, pw_ref[grp]) * ps_ref[:, lanes]).astype(jnp.bfloat16)
        out = out + _dot(p, wop_ref[lanes, :])
    return out


def _mix_prompt_kernel(h_ref, hc_ref, hp_ref, g_ref, wa_ref, wg_ref, wp_ref,
                       cw_ref, cb_ref, cn_ref, pw_ref, ps_ref, woc_ref, wop_ref,
                       o_ref, ct_ref, pt_ref, ext_c, ext_p, y_ref):
    i = pl.program_id(0)
    tm = h_ref.shape[0]

    @pl.when(i == 0)
    def _():
        ext_c[:, 0:CONV_PAD, :] = hc_ref[...]
        ext_p[:, 0:POOL_PAD, :] = hp_ref[...]

    @pl.when(i > 0)
    def _():
        ext_c[:, 0:CONV_PAD, :] = ext_c[:, tm:tm + CONV_PAD, :]
        ext_p[:, 0:POOL_PAD, :] = ext_p[:, tm:tm + POOL_PAD, :]

    h = h_ref[...]
    hn = _rmsnorm(h, g_ref[...]).astype(jnp.bfloat16)
    a = _dot(hn, wa_ref[...])
    gate = _dot(hn, wg_ref[...])
    ext_c[0, CONV_PAD:CONV_PAD + tm, :] = a * jax.nn.sigmoid(gate)
    ext_p[0, POOL_PAD:POOL_PAD + tm, :] = _dot(hn, wp_ref[...])

    o_ref[...] = _mix_tail(h, ext_c, ext_p, y_ref, cw_ref, cb_ref, cn_ref, pw_ref, ps_ref,
                           woc_ref, wop_ref, 1, tm)
    ct_ref[...] = ext_c[:, tm:tm + CONV_PAD, :]
    pt_ref[...] = ext_p[:, tm:tm + POOL_PAD, :]


def _mix_small_kernel(h_ref, sc_ref, sp_ref, g_ref, wa_ref, wg_ref, wp_ref,
                      cw_ref, cb_ref, cn_ref, pw_ref, ps_ref, woc_ref, wop_ref,
                      o_ref, nc_ref, np_ref, mc_ref, mp_ref, ext_c, ext_p, y_ref):
    n_streams, n_rows = ext_c.shape[0], ext_c.shape[1] - CONV_PAD
    n_tok = n_streams * n_rows

    ext_c[:, CONV_OFF:CONV_PAD, :] = sc_ref[...]
    ext_p[:, POOL_OFF:POOL_PAD, :] = sp_ref[...]

    h_all = h_ref[...]
    hn = _rmsnorm(h_all, g_ref[...]).astype(jnp.bfloat16)
    a = _dot(hn, wa_ref[...])
    gate = _dot(hn, wg_ref[...])
    u_conv = a * jax.nn.sigmoid(gate)
    u_pool = _dot(hn, wp_ref[...])
    mc_ref[...] = u_conv[n_tok:, :]
    mp_ref[...] = u_pool[n_tok:, :]
    ext_c[:, CONV_PAD:, :] = u_conv[:n_tok, :].reshape(n_streams, n_rows, CONV_CH)
    ext_p[:, POOL_PAD:, :] = u_pool[:n_tok, :].reshape(n_streams, n_rows, POOL_CH)

    o_ref[...] = _mix_tail(h_all[:n_tok, :], ext_c, ext_p, y_ref, cw_ref, cb_ref, cn_ref, pw_ref,
                           ps_ref, woc_ref, wop_ref, n_streams, n_rows)
    nc_ref[...] = ext_c[:, n_rows + CONV_OFF:, :]
    np_ref[...] = ext_p[:, n_rows + POOL_OFF:, :]


def _const_spec(shape):
    zeros = (0,) * len(shape)
    return pl.BlockSpec(shape, lambda *_: zeros, pipeline_mode=pl.Buffered(1))


def _mix_weight_specs():
    return [
        _const_spec((1, D_MODEL)),
        _const_spec((D_MODEL, CONV_CH)),
        _const_spec((D_MODEL, CONV_CH)),
        _const_spec((D_MODEL, POOL_CH)),
        _const_spec((CONV_K, CONV_CH)),
        _const_spec((1, CONV_CH)),
        _const_spec((1, CONV_CH)),
        _const_spec((len(POOL_WINDOWS), POOL_GROUP, POOL_GROUP)),
        _const_spec((1, POOL_CH)),
        _const_spec((CONV_CH, D_MODEL)),
        _const_spec((POOL_CH, D_MODEL)),
    ]


def _mix_prompt(h, hist_c, hist_p, weights):
    t, d = h.shape
    tm = MIX_ROWS
    assert t % tm == 0
    return pl.pallas_call(
        _mix_prompt_kernel,
        out_shape=(jax.ShapeDtypeStruct((t, d), jnp.float32),
                   jax.ShapeDtypeStruct((1, CONV_PAD, CONV_CH), jnp.float32),
                   jax.ShapeDtypeStruct((1, POOL_PAD, POOL_CH), jnp.float32)),
        grid=(t // tm,),
        in_specs=[pl.BlockSpec((tm, d), lambda i: (i, 0)),
                  _const_spec((1, CONV_PAD, CONV_CH)),
                  _const_spec((1, POOL_PAD, POOL_CH))] + _mix_weight_specs(),
        out_specs=(pl.BlockSpec((tm, d), lambda i: (i, 0)),
                   pl.BlockSpec((1, CONV_PAD, CONV_CH), lambda i: (0, 0, 0)),
                   pl.BlockSpec((1, POOL_PAD, POOL_CH), lambda i: (0, 0, 0))),
        scratch_shapes=[pltpu.VMEM((1, CONV_PAD + tm, CONV_CH), jnp.float32),
                        pltpu.VMEM((1, POOL_PAD + tm, POOL_CH), jnp.float32),
                        pltpu.VMEM((tm, CONV_CH), jnp.float32)],
        compiler_params=pltpu.CompilerParams(
            dimension_semantics=("arbitrary",),
            vmem_limit_bytes=VMEM_LIMIT_BYTES),
        name="mix_prompt",
    )(h, hist_c, hist_p, *weights)


def _mix_small(h, state_c, state_p, weights):
    t, d = h.shape
    n_streams, _, _ = state_c.shape
    n_tok = t - N_META
    n_rows = n_tok // n_streams
    assert n_rows * n_streams == n_tok and n_rows % SUBLANES == 0
    return pl.pallas_call(
        _mix_small_kernel,
        out_shape=(jax.ShapeDtypeStruct((n_tok, d), jnp.float32),
                   jax.ShapeDtypeStruct((n_streams, CONV_HIST, CONV_CH), jnp.float32),
                   jax.ShapeDtypeStruct((n_streams, POOL_HIST, POOL_CH), jnp.float32),
                   jax.ShapeDtypeStruct((N_META, CONV_CH), jnp.float32),
                   jax.ShapeDtypeStruct((N_META, POOL_CH), jnp.float32)),
        grid=(1,),
        in_specs=[_const_spec((t, d)),
                  _const_spec((n_streams, CONV_HIST, CONV_CH)),
                  _const_spec((n_streams, POOL_HIST, POOL_CH))] + _mix_weight_specs(),
        out_specs=(pl.BlockSpec((n_tok, d), lambda i: (0, 0)),
                   pl.BlockSpec((n_streams, CONV_HIST, CONV_CH), lambda i: (0, 0, 0)),
                   pl.BlockSpec((n_streams, POOL_HIST, POOL_CH), lambda i: (0, 0, 0)),
                   pl.BlockSpec((N_META, CONV_CH), lambda i: (0, 0)),
                   pl.BlockSpec((N_META, POOL_CH), lambda i: (0, 0))),
        scratch_shapes=[pltpu.VMEM((n_streams, CONV_PAD + n_rows, CONV_CH), jnp.float32),
                        pltpu.VMEM((n_streams, POOL_PAD + n_rows, POOL_CH), jnp.float32),
                        pltpu.VMEM((n_tok, CONV_CH), jnp.float32)],
        compiler_params=pltpu.CompilerParams(
            dimension_semantics=("arbitrary",),
            vmem_limit_bytes=VMEM_LIMIT_BYTES),
        name="mix_small",
    )(h, state_c, state_p, *weights)


def kernel(x_prompt, x_sample, state_conv, state_pool, meta_tokens, ffn1_norm, ffn1_w_gate, ffn1_w_up, ffn1_w_down, mix_norm, w_in, conv_w, conv_b, conv_norm, pool_w, pool_scale, w_out, ffn2_norm, ffn2_w_gate, ffn2_w_up, ffn2_w_down, final_norm):
    depth = ffn1_norm.shape[0]
    b_p, seq, d = x_prompt.shape
    b_s, seq_s, _ = x_sample.shape
    assert depth == 1 and b_p == 1
    bf = jnp.bfloat16

    xp = x_prompt.reshape(seq, d)
    xs = jnp.concatenate([x_sample.reshape(b_s * seq_s, d), meta_tokens.astype(x_sample.dtype)], axis=0)

    row = lambda v: v.reshape(1, -1)
    fg = row(final_norm)
    ffn1 = (row(ffn1_norm[0]), ffn1_w_gate[0].astype(bf), ffn1_w_up[0].astype(bf), ffn1_w_down[0].astype(bf), fg)
    ffn2 = (row(ffn2_norm[0]), ffn2_w_gate[0].astype(bf), ffn2_w_up[0].astype(bf), ffn2_w_down[0].astype(bf), fg)
    wi = w_in[0]
    wo = w_out[0]
    mixw = (row(mix_norm[0]),
            wi[:, :CONV_CH].astype(bf), wi[:, CONV_CH:2 * CONV_CH].astype(bf), wi[:, 2 * CONV_CH:].astype(bf),
            conv_w[0], row(conv_b[0]), row(conv_norm[0]),
            pool_w[0].astype(bf), row(pool_scale[0]),
            wo[:CONV_CH].astype(bf), wo[CONV_CH:].astype(bf))

    hs = _ffn(xs, *ffn1, final_norm=False, name="ffn1_small")
    hp = _ffn(xp, *ffn1, final_norm=False, name="ffn1_prompt")

    hs, new_conv_s, new_pool_s, meta_c, meta_p = _mix_small(hs, state_conv[0], state_pool[0], mixw)
    hist_c = jnp.concatenate([jnp.zeros((CONV_PAD - N_META, CONV_CH), jnp.float32), meta_c], axis=0)[None]
    hist_p = meta_p[None]
    hp, conv_tail, pool_tail = _mix_prompt(hp, hist_c, hist_p, mixw)

    ys = _ffn(hs, *ffn2, final_norm=True, name="ffn2_small")
    yp = _ffn(hp, *ffn2, final_norm=True, name="ffn2_prompt")

    y_prompt = yp.reshape(b_p, seq, d)
    y_sample = ys.reshape(b_s, seq_s, d)
    new_conv_prompt = conv_tail[:, CONV_OFF:, :][None]
    new_pool_prompt = pool_tail[:, POOL_OFF:, :][None]
    return (y_prompt, y_sample, new_conv_prompt, new_pool_prompt, new_conv_s[None], new_pool_s[None])
```

```python
import functools

import jax
import jax.numpy as jnp
from jax import lax
from jax.experimental import pallas as pl
from jax.experimental.pallas import tpu as pltpu

D_MODEL = 2048
N_META = 16
CONV_CH = 1024
POOL_CH = 1024
CONV_K = 31
CONV_HIST = CONV_K - 1
POOL_WINDOWS = (2, 4, 8, 16)
POOL_GROUP = POOL_CH // len(POOL_WINDOWS)
POOL_HIST = max(POOL_WINDOWS) - 1
EPS = 1e-6

SUBLANES = 8
LANES = 128
VMEM_LIMIT_BYTES = 60 * 1024 * 1024

SLABS = CONV_CH // LANES
assert POOL_CH == CONV_CH and POOL_GROUP % LANES == 0
SLABS_PER_GROUP = POOL_GROUP // LANES

CONV_PAD = 32
POOL_PAD = 16
CONV_OFF = CONV_PAD - CONV_HIST
POOL_OFF = POOL_PAD - POOL_HIST

FFN_ROWS = 1024
FFN_COLS = 512
MIX_ROWS = 512
MIX_SMALL_STREAMS = 8
MAX_BLOCK_ROWS = 4 * SUBLANES


def _rmsnorm(x, g):
    ms = jnp.mean(x * x, axis=-1, keepdims=True)
    return x * lax.rsqrt(ms + EPS) * g


def _dot(a, b):
    return jnp.dot(a, b, preferred_element_type=jnp.float32)


def _ffn_kernel(x_ref, g_ref, wg_ref, wu_ref, wd_ref, fg_ref, o_ref, xn_ref, *, final_norm):
    j = pl.program_id(1)

    @pl.when(j == 0)
    def _():
        x = x_ref[...]
        xn_ref[...] = _rmsnorm(x, g_ref[...]).astype(jnp.bfloat16)
        o_ref[...] = x

    xn = xn_ref[...]
    gate = _dot(xn, wg_ref[...])
    up = _dot(xn, wu_ref[...])
    hid = (0.5 * (gate * jax.nn.sigmoid(gate)) * up).astype(jnp.bfloat16)
    o_ref[...] += _dot(hid, wd_ref[...])

    if final_norm:
        @pl.when(j == pl.num_programs(1) - 1)
        def _():
            o_ref[...] = _rmsnorm(o_ref[...], fg_ref[...])


def _ffn(x, g, wg, wu, wd, fg, *, final_norm, name):
    t, d = x.shape
    f = wg.shape[1]
    tm = FFN_ROWS if t % FFN_ROWS == 0 else t
    tf = FFN_COLS
    assert t % tm == 0 and f % tf == 0
    return pl.pallas_call(
        functools.partial(_ffn_kernel, final_norm=final_norm),
        out_shape=jax.ShapeDtypeStruct((t, d), jnp.float32),
        grid=(t // tm, f // tf),
        in_specs=[
            pl.BlockSpec((tm, d), lambda i, j: (i, 0)),
            pl.BlockSpec((1, d), lambda i, j: (0, 0)),
            pl.BlockSpec((d, tf), lambda i, j: (0, j)),
            pl.BlockSpec((d, tf), lambda i, j: (0, j)),
            pl.BlockSpec((tf, d), lambda i, j: (j, 0)),
            pl.BlockSpec((1, d), lambda i, j: (0, 0)),
        ],
        out_specs=pl.BlockSpec((tm, d), lambda i, j: (i, 0)),
        scratch_shapes=[pltpu.VMEM((tm, d), jnp.bfloat16)],
        compiler_params=pltpu.CompilerParams(
            dimension_semantics=("arbitrary", "arbitrary"),
            vmem_limit_bytes=VMEM_LIMIT_BYTES),
        name=name,
    )(x, g, wg, wu, wd, fg)


def _block_rows(n_rows):
    rb = min(MAX_BLOCK_ROWS, n_rows)
    assert n_rows % rb == 0 and rb % SUBLANES == 0
    return rb, rb // SUBLANES


def _lanes(slab):
    return slice(slab * LANES, (slab + 1) * LANES)


def _causal_dwconv(ext_c, cw_ref, cb_ref, y_ref, n_streams, n_rows):
    rb, stride = _block_rows(n_rows)
    for s in range(SLABS):
        bias = cb_ref[s]
        w = [cw_ref[s, k] for k in range(CONV_K)]
        for b in range(n_streams):
            for r0 in range(0, n_rows, rb):
                acc = [bias] * stride
                for m in range(stride - 1 + CONV_K):
                    x = ext_c[b * SLABS + s, pl.ds(CONV_OFF + r0 + m, SUBLANES, stride=stride), :]
                    for j in range(stride):
                        k = m - j
                        if 0 <= k < CONV_K:
                            acc[j] = acc[j] + w[k] * x
                for j in range(stride):
                    y_ref[s, pl.ds(b * n_rows + r0 + j, SUBLANES, stride=stride), :] = acc[j]


def _pool(ext_p, p_ref, n_streams, n_rows):
    rb, stride = _block_rows(n_rows)
    for grp, window in enumerate(POOL_WINDOWS):
        for s in range(grp * SLABS_PER_GROUP, (grp + 1) * SLABS_PER_GROUP):
            for b in range(n_streams):
                for r0 in range(0, n_rows, rb):
                    for j in range(stride):
                        base = POOL_PAD + r0 + j
                        tok = ext_p[b * SLABS + s, pl.ds(base, SUBLANES, stride=stride), :]
                        tot = tok
                        for i in range(1, window):
                            tot = tot + ext_p[b * SLABS + s, pl.ds(base - i, SUBLANES, stride=stride), :]
                        p_ref[s, pl.ds(b * n_rows + r0 + j, SUBLANES, stride=stride), :] = (
                            tot * (1.0 / window) - tok)


def _mix_tail(h, ext_c, ext_p, y_ref, p_ref, cw_ref, cb_ref, cn_ref, pw_ref, ps_ref, woc_ref, wop_ref,
              n_streams, n_rows):
    _causal_dwconv(ext_c, cw_ref, cb_ref, y_ref, n_streams, n_rows)
    _pool(ext_p, p_ref, n_streams, n_rows)
    y = jnp.concatenate([y_ref[s] for s in range(SLABS)], axis=-1)
    cn = _rmsnorm(y, cn_ref[...])
    c = (cn * jax.nn.sigmoid(cn)).astype(jnp.bfloat16)
    out = h + _dot(c, woc_ref[...])
    for grp in range(len(POOL_WINDOWS)):
        lanes = slice(grp * POOL_GROUP, (grp + 1) * POOL_GROUP)
        slabs = range(grp * SLABS_PER_GROUP, (grp + 1) * SLABS_PER_GROUP)
        p = jnp.concatenate([p_ref[s] for s in slabs], axis=-1).astype(jnp.bfloat16)
        p = (_dot(p, pw_ref[grp]) * ps_ref[:, lanes]).astype(jnp.bfloat16)
        out = out + _dot(p, wop_ref[lanes, :])
    return out


def _mix_prompt_kernel(h_ref, hc_ref, hp_ref, g_ref, wa_ref, wg_ref, wp_ref,
                       cw_ref, cb_ref, cn_ref, pw_ref, ps_ref, woc_ref, wop_ref,
                       o_ref, ct_ref, pt_ref, ext_c, ext_p, y_ref, p_ref):
    i = pl.program_id(0)
    tm = h_ref.shape[0]

    @pl.when(i == 0)
    def _():
        ext_c[:, 0:CONV_PAD, :] = hc_ref[...]
        ext_p[:, 0:POOL_PAD, :] = hp_ref[...]

    @pl.when(i > 0)
    def _():
        ext_c[:, 0:CONV_PAD, :] = ext_c[:, tm:tm + CONV_PAD, :]
        ext_p[:, 0:POOL_PAD, :] = ext_p[:, tm:tm + POOL_PAD, :]

    h = h_ref[...]
    u_conv, u_pool = _mix_inputs(h, g_ref, wa_ref, wg_ref, wp_ref)
    for s in range(SLABS):
        ext_c[s, CONV_PAD:CONV_PAD + tm, :] = u_conv[:, _lanes(s)]
        ext_p[s, POOL_PAD:POOL_PAD + tm, :] = u_pool[:, _lanes(s)]

    o_ref[...] = _mix_tail(h, ext_c, ext_p, y_ref, p_ref, cw_ref, cb_ref, cn_ref, pw_ref, ps_ref,
                           woc_ref, wop_ref, 1, tm)
    ct_ref[...] = ext_c[:, tm:tm + CONV_PAD, :]
    pt_ref[...] = ext_p[:, tm:tm + POOL_PAD, :]


def _mix_inputs(h, g_ref, wa_ref, wg_ref, wp_ref):
    hn = _rmsnorm(h, g_ref[...]).astype(jnp.bfloat16)
    a = _dot(hn, wa_ref[...])
    gate = _dot(hn, wg_ref[...])
    return a * jax.nn.sigmoid(gate), _dot(hn, wp_ref[...])


def _mix_small_kernel(h_ref, hm_ref, sc_ref, sp_ref, g_ref, wa_ref, wg_ref, wp_ref,
                      cw_ref, cb_ref, cn_ref, pw_ref, ps_ref, woc_ref, wop_ref,
                      o_ref, nc_ref, np_ref, mc_ref, mp_ref, ext_c, ext_p, y_ref, p_ref):
    n_streams = sc_ref.shape[0]
    n_rows = ext_c.shape[1] - CONV_PAD

    @pl.when(pl.program_id(0) == 0)
    def _():
        m_conv, m_pool = _mix_inputs(hm_ref[...], g_ref, wa_ref, wg_ref, wp_ref)
        for s in range(SLABS):
            mc_ref[s] = m_conv[:, _lanes(s)]
            mp_ref[s] = m_pool[:, _lanes(s)]

    h = h_ref[...]
    u_conv, u_pool = _mix_inputs(h, g_ref, wa_ref, wg_ref, wp_ref)
    for s in range(SLABS):
        for b in range(n_streams):
            rows = slice(b * n_rows, (b + 1) * n_rows)
            ext_c[b * SLABS + s, CONV_OFF:CONV_PAD, :] = sc_ref[b, :, _lanes(s)]
            ext_p[b * SLABS + s, POOL_OFF:POOL_PAD, :] = sp_ref[b, :, _lanes(s)]
            ext_c[b * SLABS + s, CONV_PAD:, :] = u_conv[rows, _lanes(s)]
            ext_p[b * SLABS + s, POOL_PAD:, :] = u_pool[rows, _lanes(s)]

    o_ref[...] = _mix_tail(h, ext_c, ext_p, y_ref, p_ref, cw_ref, cb_ref, cn_ref, pw_ref,
                           ps_ref, woc_ref, wop_ref, n_streams, n_rows)
    for s in range(SLABS):
        for b in range(n_streams):
            nc_ref[b, :, _lanes(s)] = ext_c[b * SLABS + s, n_rows + CONV_OFF:, :]
            np_ref[b, :, _lanes(s)] = ext_p[b * SLABS + s, n_rows + POOL_OFF:, :]


def _const_spec(shape):
    zeros = (0,) * len(shape)
    return pl.BlockSpec(shape, lambda *_: zeros, pipeline_mode=pl.Buffered(1))


def _mix_weight_specs():
    return [
        _const_spec((1, D_MODEL)),
        _const_spec((D_MODEL, CONV_CH)),
        _const_spec((D_MODEL, CONV_CH)),
        _const_spec((D_MODEL, POOL_CH)),
        _const_spec((SLABS, CONV_K, SUBLANES, LANES)),
        _const_spec((SLABS, SUBLANES, LANES)),
        _const_spec((1, CONV_CH)),
        _const_spec((len(POOL_WINDOWS), POOL_GROUP, POOL_GROUP)),
        _const_spec((1, POOL_CH)),
        _const_spec((CONV_CH, D_MODEL)),
        _const_spec((POOL_CH, D_MODEL)),
    ]


def _mix_prompt(h, hist_c, hist_p, weights):
    t, d = h.shape
    tm = MIX_ROWS
    assert t % tm == 0
    return pl.pallas_call(
        _mix_prompt_kernel,
        out_shape=(jax.ShapeDtypeStruct((t, d), jnp.float32),
                   jax.ShapeDtypeStruct((SLABS, CONV_PAD, LANES), jnp.float32),
                   jax.ShapeDtypeStruct((SLABS, POOL_PAD, LANES), jnp.float32)),
        grid=(t // tm,),
        in_specs=[pl.BlockSpec((tm, d), lambda i: (i, 0)),
                  _const_spec((SLABS, CONV_PAD, LANES)),
                  _const_spec((SLABS, POOL_PAD, LANES))] + _mix_weight_specs(),
        out_specs=(pl.BlockSpec((tm, d), lambda i: (i, 0)),
                   pl.BlockSpec((SLABS, CONV_PAD, LANES), lambda i: (0, 0, 0)),
                   pl.BlockSpec((SLABS, POOL_PAD, LANES), lambda i: (0, 0, 0))),
        scratch_shapes=[pltpu.VMEM((SLABS, CONV_PAD + tm, LANES), jnp.float32),
                        pltpu.VMEM((SLABS, POOL_PAD + tm, LANES), jnp.float32),
                        pltpu.VMEM((SLABS, tm, LANES), jnp.float32),
                        pltpu.VMEM((SLABS, tm, LANES), jnp.float32)],
        compiler_params=pltpu.CompilerParams(
            dimension_semantics=("arbitrary",),
            vmem_limit_bytes=VMEM_LIMIT_BYTES),
        name="mix_prompt",
    )(h, hist_c, hist_p, *weights)


def _mix_small(h, state_c, state_p, weights):
    t, d = h.shape
    n_streams, _, _ = state_c.shape
    n_tok = t - N_META
    n_rows = n_tok // n_streams
    grp = MIX_SMALL_STREAMS
    tm = grp * n_rows
    assert n_rows * n_streams == n_tok and n_rows % SUBLANES == 0
    assert n_streams % grp == 0 and n_tok % N_META == 0
    return pl.pallas_call(
        _mix_small_kernel,
        out_shape=(jax.ShapeDtypeStruct((n_tok, d), jnp.float32),
                   jax.ShapeDtypeStruct((n_streams, CONV_HIST, CONV_CH), jnp.float32),
                   jax.ShapeDtypeStruct((n_streams, POOL_HIST, POOL_CH), jnp.float32),
                   jax.ShapeDtypeStruct((SLABS, N_META, LANES), jnp.float32),
                   jax.ShapeDtypeStruct((SLABS, N_META, LANES), jnp.float32)),
        grid=(n_streams // grp,),
        in_specs=[pl.BlockSpec((tm, d), lambda i: (i, 0)),
                  pl.BlockSpec((N_META, d), lambda i: (n_tok // N_META, 0), pipeline_mode=pl.Buffered(1)),
                  pl.BlockSpec((grp, CONV_HIST, CONV_CH), lambda i: (i, 0, 0)),
                  pl.BlockSpec((grp, POOL_HIST, POOL_CH), lambda i: (i, 0, 0))] + _mix_weight_specs(),
        out_specs=(pl.BlockSpec((tm, d), lambda i: (i, 0)),
                   pl.BlockSpec((grp, CONV_HIST, CONV_CH), lambda i: (i, 0, 0)),
                   pl.BlockSpec((grp, POOL_HIST, POOL_CH), lambda i: (i, 0, 0)),
                   pl.BlockSpec((SLABS, N_META, LANES), lambda i: (0, 0, 0)),
                   pl.BlockSpec((SLABS, N_META, LANES), lambda i: (0, 0, 0))),
        scratch_shapes=[pltpu.VMEM((grp * SLABS, CONV_PAD + n_rows, LANES), jnp.float32),
                        pltpu.VMEM((grp * SLABS, POOL_PAD + n_rows, LANES), jnp.float32),
                        pltpu.VMEM((SLABS, tm, LANES), jnp.float32),
                        pltpu.VMEM((SLABS, tm, LANES), jnp.float32)],
        compiler_params=pltpu.CompilerParams(
            dimension_semantics=("arbitrary",),
            vmem_limit_bytes=VMEM_LIMIT_BYTES),
        name="mix_small",
    )(h, h, state_c, state_p, *weights)


def _to_slabs(v):
    return v.reshape(v.shape[0], SLABS, LANES).transpose(1, 0, 2)


def _from_slabs(v):
    return v.transpose(1, 0, 2).reshape(v.shape[1], SLABS * LANES)


def kernel(x_prompt, x_sample, state_conv, state_pool, meta_tokens, ffn1_norm, ffn1_w_gate, ffn1_w_up, ffn1_w_down, mix_norm, w_in, conv_w, conv_b, conv_norm, pool_w, pool_scale, w_out, ffn2_norm, ffn2_w_gate, ffn2_w_up, ffn2_w_down, final_norm):
    depth = ffn1_norm.shape[0]
    b_p, seq, d = x_prompt.shape
    b_s, seq_s, _ = x_sample.shape
    assert depth == 1 and b_p == 1
    bf = jnp.bfloat16

    xp = x_prompt.reshape(seq, d)
    xs = jnp.concatenate([x_sample.reshape(b_s * seq_s, d), meta_tokens.astype(x_sample.dtype)], axis=0)

    row = lambda v: v.reshape(1, -1)
    fg = row(final_norm)
    ffn1 = (row(ffn1_norm[0]), ffn1_w_gate[0].astype(bf), ffn1_w_up[0].astype(bf), ffn1_w_down[0].astype(bf), fg)
    ffn2 = (row(ffn2_norm[0]), ffn2_w_gate[0].astype(bf), ffn2_w_up[0].astype(bf), ffn2_w_down[0].astype(bf), fg)
    wi = w_in[0]
    wo = w_out[0]
    mixw = (row(mix_norm[0]),
            wi[:, :CONV_CH].astype(bf), wi[:, CONV_CH:2 * CONV_CH].astype(bf), wi[:, 2 * CONV_CH:].astype(bf),
            jnp.broadcast_to(_to_slabs(conv_w[0])[:, :, None, :], (SLABS, CONV_K, SUBLANES, LANES)),
            jnp.broadcast_to(_to_slabs(row(conv_b[0])), (SLABS, SUBLANES, LANES)),
            row(conv_norm[0]),
            pool_w[0].astype(bf), row(pool_scale[0]),
            wo[:CONV_CH].astype(bf), wo[CONV_CH:].astype(bf))

    hs = _ffn(xs, *ffn1, final_norm=False, name="ffn1_small")
    hp = _ffn(xp, *ffn1, final_norm=False, name="ffn1_prompt")

    hs, new_conv_s, new_pool_s, meta_c, meta_p = _mix_small(hs, state_conv[0], state_pool[0], mixw)
    hist_c = jnp.concatenate([jnp.zeros((SLABS, CONV_PAD - N_META, LANES), jnp.float32), meta_c], axis=1)
    hp, conv_tail, pool_tail = _mix_prompt(hp, hist_c, meta_p, mixw)

    ys = _ffn(hs, *ffn2, final_norm=True, name="ffn2_small")
    yp = _ffn(hp, *ffn2, final_norm=True, name="ffn2_prompt")

    y_prompt = yp.reshape(b_p, seq, d)
    y_sample = ys.reshape(b_s, seq_s, d)
    new_conv_prompt = _from_slabs(conv_tail[:, CONV_OFF:, :])[None, None]
    new_pool_prompt = _from_slabs(pool_tail[:, POOL_OFF:, :])[None, None]
    return (y_prompt, y_sample, new_conv_prompt, new_pool_prompt, new_conv_s[None], new_pool_s[None])
```

```python
import functools
from typing import Any, NamedTuple

import jax
import jax.numpy as jnp
from jax import lax
from jax.experimental import pallas as pl
from jax.experimental.pallas import tpu as pltpu

D_MODEL = 2048
N_META = 16
CONV_CH = 1024
POOL_CH = 1024
MIX_WIDTH = CONV_CH + POOL_CH
CONV_K = 31
CONV_HIST = CONV_K - 1
POOL_WINDOWS = (2, 4, 8, 16)
POOL_GROUP = POOL_CH // len(POOL_WINDOWS)
POOL_HIST = max(POOL_WINDOWS) - 1
EPS = 1e-6

SUBLANES = 8
LANES = 128
VMEM_LIMIT_BYTES = 60 * 1024 * 1024
WEIGHT_WINDOW = 512

SLABS = CONV_CH // LANES
assert POOL_CH == CONV_CH and POOL_GROUP % LANES == 0
SLABS_PER_GROUP = POOL_GROUP // LANES

CONV_PAD = 32
POOL_PAD = 16
CONV_OFF = CONV_PAD - CONV_HIST
POOL_OFF = POOL_PAD - POOL_HIST

FFN_ROWS = 1024
FFN_COLS = 512
MIX_ROWS = 512
MIX_SUBTILES = 2
MIX_SMALL_STREAMS = 8
MAX_BLOCK_ROWS = 4 * SUBLANES


def _rmsnorm(x, g):
    ms = jnp.mean(x * x, axis=-1, keepdims=True)
    return x * lax.rsqrt(ms + EPS) * g


def _dot(a, b):
    return jnp.dot(a, b, preferred_element_type=jnp.float32)


def _ffn_kernel(x_ref, g_ref, wg_ref, wu_ref, wd_ref, fg_ref, *refs, final_norm, n_cast):
    cast_in, o_ref, cast_out, xn_ref = refs[:n_cast], refs[n_cast], refs[n_cast + 1:-1], refs[-1]
    j = pl.program_id(1)

    @pl.when(j == 0)
    def _():
        x = x_ref[...]
        xn_ref[...] = _rmsnorm(x, g_ref[...]).astype(jnp.bfloat16)
        o_ref[...] = x

    xn = xn_ref[...]
    gate = _dot(xn, wg_ref[...])
    up = _dot(xn, wu_ref[...])
    hid = (0.5 * (gate * jax.nn.sigmoid(gate)) * up).astype(jnp.bfloat16)
    o_ref[...] += _dot(hid, wd_ref[...])

    for src, dst in zip(cast_in, cast_out):
        dst[...] = src[...].astype(jnp.bfloat16)

    if final_norm:
        @pl.when(j == pl.num_programs(1) - 1)
        def _():
            o_ref[...] = _rmsnorm(o_ref[...], fg_ref[...])


def _cast_spec(shape, ni, nj):
    r, c = shape
    if r % ni == 0 and c % nj == 0 and (c // nj) % LANES == 0:
        return pl.BlockSpec((r // ni, c // nj), lambda i, j: (i, j))
    assert r % nj == 0 and c % ni == 0 and (c // ni) % LANES == 0
    return pl.BlockSpec((r // nj, c // ni), lambda i, j: (j, i))


def _ffn(x, g, wg, wu, wd, fg, cast=(), *, final_norm, name):
    t, d = x.shape
    f = wg.shape[1]
    tm = FFN_ROWS if t % FFN_ROWS == 0 else t
    tf = FFN_COLS
    assert t % tm == 0 and f % tf == 0
    grid = (t // tm, f // tf)
    cast_specs = [_cast_spec(w.shape, *grid) for w in cast]
    out = pl.pallas_call(
        functools.partial(_ffn_kernel, final_norm=final_norm, n_cast=len(cast)),
        out_shape=[jax.ShapeDtypeStruct((t, d), jnp.float32)]
        + [jax.ShapeDtypeStruct(w.shape, jnp.bfloat16) for w in cast],
        grid=grid,
        in_specs=[
            pl.BlockSpec((tm, d), lambda i, j: (i, 0)),
            pl.BlockSpec((1, d), lambda i, j: (0, 0)),
            pl.BlockSpec((d, tf), lambda i, j: (0, j)),
            pl.BlockSpec((d, tf), lambda i, j: (0, j)),
            pl.BlockSpec((tf, d), lambda i, j: (j, 0)),
            pl.BlockSpec((1, d), lambda i, j: (0, 0)),
        ] + cast_specs,
        out_specs=[pl.BlockSpec((tm, d), lambda i, j: (i, 0))] + cast_specs,
        scratch_shapes=[pltpu.VMEM((tm, d), jnp.bfloat16)],
        compiler_params=pltpu.CompilerParams(
            dimension_semantics=("arbitrary", "arbitrary"),
            vmem_limit_bytes=VMEM_LIMIT_BYTES),
        name=name,
    )(x, g, wg, wu, wd, fg, *cast)
    return out[0] if not cast else out


class _MixWeights(NamedTuple):
    norm: Any
    w_in: Any
    conv_w: Any
    conv_b: Any
    conv_norm: Any
    pool_w: Any
    pool_scale: Any
    w_out: Any


def _mix_weight_operands(w):
    def windows(v):
        k, n = v.shape
        assert n % WEIGHT_WINDOW == 0
        return [(v, pl.BlockSpec((k, WEIGHT_WINDOW), functools.partial(lambda c, *_: (0, c), c),
                                 pipeline_mode=pl.Buffered(1))) for c in range(n // WEIGHT_WINDOW)]

    pairs = []
    for name, v in w._asdict().items():
        pairs += windows(v) if name in ("w_in", "w_out") else [(v, _const_spec(v.shape))]
    return [v for v, _ in pairs], [s for _, s in pairs]


N_IN_WINDOWS = (2 * CONV_CH + POOL_CH) // WEIGHT_WINDOW
N_OUT_WINDOWS = D_MODEL // WEIGHT_WINDOW
N_MIX_WEIGHT_REFS = len(_MixWeights._fields) - 2 + N_IN_WINDOWS + N_OUT_WINDOWS


def _mix_weight_refs(refs):
    refs = list(refs)
    assert len(refs) == N_MIX_WEIGHT_REFS
    take = lambda n: tuple(refs.pop(0) for _ in range(n))
    return _MixWeights(**{name: take(N_IN_WINDOWS) if name == "w_in" else
                          take(N_OUT_WINDOWS) if name == "w_out" else refs.pop(0)
                          for name in _MixWeights._fields})


class _MixScratch(NamedTuple):
    ext_c: Any
    ext_p: Any
    y: Any
    p: Any
    hn: Any
    cp: Any


def _mix_scratch_shapes(n_streams, n_rows, n_sub=None):
    rows = n_streams * n_rows
    lead = () if n_sub is None else (n_sub,)
    new_rows = n_rows * (n_sub or 1)
    return [pltpu.VMEM((n_streams * SLABS, CONV_PAD + new_rows, LANES), jnp.float32),
            pltpu.VMEM((n_streams * SLABS, POOL_PAD + new_rows, LANES), jnp.float32),
            pltpu.VMEM(lead + (SLABS, rows, LANES), jnp.float32),
            pltpu.VMEM(lead + (SLABS, rows, LANES), jnp.float32),
            pltpu.VMEM(lead + (rows, D_MODEL), jnp.bfloat16),
            pltpu.VMEM(lead + (rows, MIX_WIDTH), jnp.bfloat16)]


def _block_rows(n_rows):
    rb = min(MAX_BLOCK_ROWS, n_rows)
    assert n_rows % rb == 0 and rb % SUBLANES == 0
    return rb, rb // SUBLANES


def _lanes(slab):
    return slice(slab * LANES, (slab + 1) * LANES)


def _mix_norm(h, w, hn_ref):
    hn_ref[...] = _rmsnorm(h, w.norm[...]).astype(jnp.bfloat16)


def _mix_inputs(w, hn_ref):
    nc = CONV_CH // WEIGHT_WINDOW
    u_conv = []
    for c in range(nc):
        a = _dot(hn_ref[...], w.w_in[c][...])
        gate = _dot(hn_ref[...], w.w_in[nc + c][...])
        u_conv.append(a * jax.nn.sigmoid(gate))
    u_pool = [_dot(hn_ref[...], w_ref[...]) for w_ref in w.w_in[2 * nc:]]
    return u_conv, u_pool


def _slab(chunks, s, rows=slice(None)):
    per = WEIGHT_WINDOW // LANES
    return chunks[s // per][rows, _lanes(s % per)]


def _causal_dwconv(ext_c, w, y_ref, n_streams, n_rows, row0):
    rb, stride = _block_rows(n_rows)
    for s in range(SLABS):
        bias = w.conv_b[s]
        taps = [w.conv_w[s, k] for k in range(CONV_K)]
        for b in range(n_streams):
            for r0 in range(0, n_rows, rb):
                acc = [bias] * stride
                for m in range(stride - 1 + CONV_K):
                    x = ext_c[b * SLABS + s, pl.ds(CONV_OFF + row0 + r0 + m, SUBLANES, stride=stride), :]
                    for j in range(stride):
                        k = m - j
                        if 0 <= k < CONV_K:
                            acc[j] = acc[j] + taps[k] * x
                for j in range(stride):
                    y_ref[s, pl.ds(b * n_rows + r0 + j, SUBLANES, stride=stride), :] = acc[j]


def _pool(ext_p, p_ref, n_streams, n_rows, row0):
    rb, stride = _block_rows(n_rows)
    for grp, window in enumerate(POOL_WINDOWS):
        for s in range(grp * SLABS_PER_GROUP, (grp + 1) * SLABS_PER_GROUP):
            for b in range(n_streams):
                for r0 in range(0, n_rows, rb):
                    for j in range(stride):
                        base = POOL_PAD + row0 + r0 + j
                        tok = ext_p[b * SLABS + s, pl.ds(base, SUBLANES, stride=stride), :]
                        tot = tok
                        for i in range(1, window):
                            tot = tot + ext_p[b * SLABS + s, pl.ds(base - i, SUBLANES, stride=stride), :]
                        p_ref[s, pl.ds(b * n_rows + r0 + j, SUBLANES, stride=stride), :] = (
                            tot * (1.0 / window) - tok)


def _mix_tail(h, w, sc, n_streams, n_rows, row0=0):
    _causal_dwconv(sc.ext_c, w, sc.y, n_streams, n_rows, row0)
    _pool(sc.ext_p, sc.p, n_streams, n_rows, row0)
    y = jnp.concatenate([sc.y[s] for s in range(SLABS)], axis=-1)
    cn = _rmsnorm(y, w.conv_norm[...])
    sc.cp[:, :CONV_CH] = (cn * jax.nn.sigmoid(cn)).astype(jnp.bfloat16)
    for grp in range(len(POOL_WINDOWS)):
        lanes = slice(grp * POOL_GROUP, (grp + 1) * POOL_GROUP)
        slabs = range(grp * SLABS_PER_GROUP, (grp + 1) * SLABS_PER_GROUP)
        p = jnp.concatenate([sc.p[s] for s in slabs], axis=-1).astype(jnp.bfloat16)
        sc.cp[:, CONV_CH + grp * POOL_GROUP:CONV_CH + (grp + 1) * POOL_GROUP] = (
            _dot(p, w.pool_w[grp]) * w.pool_scale[:, lanes]).astype(jnp.bfloat16)
    return jnp.concatenate(
        [h[:, c * WEIGHT_WINDOW:(c + 1) * WEIGHT_WINDOW] + _dot(sc.cp[...], w_ref[...])
         for c, w_ref in enumerate(w.w_out)], axis=-1)


N_MIX_SCRATCH = len(_MixScratch._fields)


def _mix_prompt_kernel(h_ref, hc_ref, hp_ref, *refs):
    w = _mix_weight_refs(refs[:N_MIX_WEIGHT_REFS])
    o_ref, ct_ref, pt_ref = refs[N_MIX_WEIGHT_REFS:-N_MIX_SCRATCH]
    sc = _MixScratch(*refs[-N_MIX_SCRATCH:])
    i = pl.program_id(0)
    tm = h_ref.shape[0]

    @pl.when(i == 0)
    def _():
        sc.ext_c[:, 0:CONV_PAD, :] = hc_ref[...]
        sc.ext_p[:, 0:POOL_PAD, :] = hp_ref[...]

    @pl.when(i > 0)
    def _():
        sc.ext_c[:, 0:CONV_PAD, :] = sc.ext_c[:, tm:tm + CONV_PAD, :]
        sc.ext_p[:, 0:POOL_PAD, :] = sc.ext_p[:, tm:tm + POOL_PAD, :]

    n_sub = sc.y.shape[0]
    sub = tm // n_sub

    def project_in(k):
        u_conv, u_pool = _mix_inputs(w, sc.hn.at[k])
        for s in range(SLABS):
            sc.ext_c[s, CONV_PAD + k * sub:CONV_PAD + (k + 1) * sub, :] = _slab(u_conv, s)
            sc.ext_p[s, POOL_PAD + k * sub:POOL_PAD + (k + 1) * sub, :] = _slab(u_pool, s)

    def mix_out(k):
        rows = slice(k * sub, (k + 1) * sub)
        sub_sc = sc._replace(y=sc.y.at[k], p=sc.p.at[k], cp=sc.cp.at[k])
        o_ref[rows, :] = _mix_tail(h_ref[rows, :], w, sub_sc, 1, sub, row0=k * sub)

    for k in range(n_sub):
        _mix_norm(h_ref[k * sub:(k + 1) * sub, :], w, sc.hn.at[k])
    project_in(0)
    for k in range(1, n_sub):
        project_in(k)
        mix_out(k - 1)
    mix_out(n_sub - 1)
    ct_ref[...] = sc.ext_c[:, tm:tm + CONV_PAD, :]
    pt_ref[...] = sc.ext_p[:, tm:tm + POOL_PAD, :]


def _mix_small_kernel(h_ref, hm_ref, sc_ref, sp_ref, *refs):
    w = _mix_weight_refs(refs[:N_MIX_WEIGHT_REFS])
    o_ref, nc_ref, np_ref, mc_ref, mp_ref = refs[N_MIX_WEIGHT_REFS:-N_MIX_SCRATCH]
    sc = _MixScratch(*refs[-N_MIX_SCRATCH:])
    n_streams = sc_ref.shape[0]
    n_rows = sc.ext_c.shape[1] - CONV_PAD

    @pl.when(pl.program_id(0) == 0)
    def _():
        _mix_norm(hm_ref[...], w, sc.hn.at[pl.ds(0, N_META)])
        m_conv, m_pool = _mix_inputs(w, sc.hn.at[pl.ds(0, N_META)])
        for s in range(SLABS):
            mc_ref[s] = _slab(m_conv, s)
            mp_ref[s] = _slab(m_pool, s)

    h = h_ref[...]
    _mix_norm(h, w, sc.hn)
    u_conv, u_pool = _mix_inputs(w, sc.hn)
    for s in range(SLABS):
        for b in range(n_streams):
            rows = slice(b * n_rows, (b + 1) * n_rows)
            sc.ext_c[b * SLABS + s, CONV_OFF:CONV_PAD, :] = sc_ref[b, :, _lanes(s)]
            sc.ext_p[b * SLABS + s, POOL_OFF:POOL_PAD, :] = sp_ref[b, :, _lanes(s)]
            sc.ext_c[b * SLABS + s, CONV_PAD:, :] = _slab(u_conv, s, rows)
            sc.ext_p[b * SLABS + s, POOL_PAD:, :] = _slab(u_pool, s, rows)

    o_ref[...] = _mix_tail(h, w, sc, n_streams, n_rows)
    for s in range(SLABS):
        for b in range(n_streams):
            nc_ref[b, :, _lanes(s)] = sc.ext_c[b * SLABS + s, n_rows + CONV_OFF:, :]
            np_ref[b, :, _lanes(s)] = sc.ext_p[b * SLABS + s, n_rows + POOL_OFF:, :]


def _const_spec(shape):
    zeros = (0,) * len(shape)
    return pl.BlockSpec(shape, lambda *_: zeros, pipeline_mode=pl.Buffered(1))


def _mix_prompt(h, hist_c, hist_p, weights):
    t, d = h.shape
    tm = MIX_ROWS
    assert t % tm == 0 and tm % MIX_SUBTILES == 0
    w_args, w_specs = _mix_weight_operands(weights)
    return pl.pallas_call(
        _mix_prompt_kernel,
        out_shape=(jax.ShapeDtypeStruct((t, d), jnp.float32),
                   jax.ShapeDtypeStruct((SLABS, CONV_PAD, LANES), jnp.float32),
                   jax.ShapeDtypeStruct((SLABS, POOL_PAD, LANES), jnp.float32)),
        grid=(t // tm,),
        in_specs=[pl.BlockSpec((tm, d), lambda i: (i, 0)),
                  _const_spec((SLABS, CONV_PAD, LANES)),
                  _const_spec((SLABS, POOL_PAD, LANES))] + w_specs,
        out_specs=(pl.BlockSpec((tm, d), lambda i: (i, 0)),
                   pl.BlockSpec((SLABS, CONV_PAD, LANES), lambda i: (0, 0, 0)),
                   pl.BlockSpec((SLABS, POOL_PAD, LANES), lambda i: (0, 0, 0))),
        scratch_shapes=_mix_scratch_shapes(1, tm // MIX_SUBTILES, MIX_SUBTILES),
        compiler_params=pltpu.CompilerParams(
            dimension_semantics=("arbitrary",),
            vmem_limit_bytes=VMEM_LIMIT_BYTES),
        name="mix_prompt",
    )(h, hist_c, hist_p, *w_args)


def _mix_small(h, state_c, state_p, weights):
    t, d = h.shape
    n_streams, _, _ = state_c.shape
    n_tok = t - N_META
    n_rows = n_tok // n_streams
    grp = MIX_SMALL_STREAMS
    tm = grp * n_rows
    assert n_rows * n_streams == n_tok and n_rows % SUBLANES == 0
    assert n_streams % grp == 0 and n_tok % N_META == 0 and tm >= N_META
    w_args, w_specs = _mix_weight_operands(weights)
    return pl.pallas_call(
        _mix_small_kernel,
        out_shape=(jax.ShapeDtypeStruct((n_tok, d), jnp.float32),
                   jax.ShapeDtypeStruct((n_streams, CONV_HIST, CONV_CH), jnp.float32),
                   jax.ShapeDtypeStruct((n_streams, POOL_HIST, POOL_CH), jnp.float32),
                   jax.ShapeDtypeStruct((SLABS, N_META, LANES), jnp.float32),
                   jax.ShapeDtypeStruct((SLABS, N_META, LANES), jnp.float32)),
        grid=(n_streams // grp,),
        in_specs=[pl.BlockSpec((tm, d), lambda i: (i, 0)),
                  pl.BlockSpec((N_META, d), lambda i: (n_tok // N_META, 0), pipeline_mode=pl.Buffered(1)),
                  pl.BlockSpec((grp, CONV_HIST, CONV_CH), lambda i: (i, 0, 0)),
                  pl.BlockSpec((grp, POOL_HIST, POOL_CH), lambda i: (i, 0, 0))]
        + w_specs,
        out_specs=(pl.BlockSpec((tm, d), lambda i: (i, 0)),
                   pl.BlockSpec((grp, CONV_HIST, CONV_CH), lambda i: (i, 0, 0)),
                   pl.BlockSpec((grp, POOL_HIST, POOL_CH), lambda i: (i, 0, 0)),
                   pl.BlockSpec((SLABS, N_META, LANES), lambda i: (0, 0, 0)),
                   pl.BlockSpec((SLABS, N_META, LANES), lambda i: (0, 0, 0))),
        scratch_shapes=_mix_scratch_shapes(grp, n_rows),
        compiler_params=pltpu.CompilerParams(
            dimension_semantics=("arbitrary",),
            vmem_limit_bytes=VMEM_LIMIT_BYTES),
        name="mix_small",
    )(h, h, state_c, state_p, *w_args)


def _to_slabs(v):
    return v.reshape(v.shape[0], SLABS, LANES).transpose(1, 0, 2)


def _from_slabs(v):
    return v.transpose(1, 0, 2).reshape(v.shape[1], SLABS * LANES)


def kernel(x_prompt, x_sample, state_conv, state_pool, meta_tokens, ffn1_norm, ffn1_w_gate, ffn1_w_up, ffn1_w_down, mix_norm, w_in, conv_w, conv_b, conv_norm, pool_w, pool_scale, w_out, ffn2_norm, ffn2_w_gate, ffn2_w_up, ffn2_w_down, final_norm):
    depth = ffn1_norm.shape[0]
    b_p, seq, d = x_prompt.shape
    b_s, seq_s, _ = x_sample.shape
    assert depth == 1 and b_p == 1
    bf = jnp.bfloat16

    xp = x_prompt.reshape(seq, d)
    xs = jnp.concatenate([x_sample.reshape(b_s * seq_s, d), meta_tokens.astype(x_sample.dtype)], axis=0)

    row = lambda v: v.reshape(1, -1)
    fg = row(final_norm)
    ffn1 = (row(ffn1_norm[0]), ffn1_w_gate[0].astype(bf), ffn1_w_up[0].astype(bf), ffn1_w_down[0].astype(bf), fg)
    mixw = _MixWeights(
        norm=row(mix_norm[0]),
        w_in=w_in[0].astype(bf),
        conv_w=jnp.broadcast_to(_to_slabs(conv_w[0])[:, :, None, :], (SLABS, CONV_K, SUBLANES, LANES)),
        conv_b=jnp.broadcast_to(_to_slabs(row(conv_b[0])), (SLABS, SUBLANES, LANES)),
        conv_norm=row(conv_norm[0]),
        pool_w=pool_w[0].astype(bf),
        pool_scale=row(pool_scale[0]),
        w_out=w_out[0].astype(bf))

    hs = _ffn(xs, *ffn1, final_norm=False, name="ffn1_small")
    hp, wg2, wu2, wd2 = _ffn(xp, *ffn1, cast=(ffn2_w_gate[0], ffn2_w_up[0], ffn2_w_down[0]),
                             final_norm=False, name="ffn1_prompt")
    ffn2 = (row(ffn2_norm[0]), wg2, wu2, wd2, fg)

    hs, new_conv_s, new_pool_s, meta_c, meta_p = _mix_small(hs, state_conv[0], state_pool[0], mixw)
    hist_c = jnp.concatenate([jnp.zeros((SLABS, CONV_PAD - N_META, LANES), jnp.float32), meta_c], axis=1)
    hp, conv_tail, pool_tail = _mix_prompt(hp, hist_c, meta_p, mixw)

    ys = _ffn(hs, *ffn2, final_norm=True, name="ffn2_small")
    yp = _ffn(hp, *ffn2, final_norm=True, name="ffn2_prompt")

    y_prompt = yp.reshape(b_p, seq, d)
    y_sample = ys.reshape(b_s, seq_s, d)
    new_conv_prompt = _from_slabs(conv_tail[:, CONV_OFF:, :])[None, None]
    new_pool_prompt = _from_slabs(pool_tail[:, POOL_OFF:, :])[None, None]
    return (y_prompt, y_sample, new_conv_prompt, new_pool_prompt, new_conv_s[None], new_pool_s[None])
```

```python
import functools
from typing import Any, NamedTuple

import jax
import jax.numpy as jnp
from jax import lax
from jax.experimental import pallas as pl
from jax.experimental.pallas import tpu as pltpu

D_MODEL = 2048
N_META = 16
CONV_CH = 1024
POOL_CH = 1024
MIX_WIDTH = CONV_CH + POOL_CH
CONV_K = 31
CONV_HIST = CONV_K - 1
POOL_WINDOWS = (2, 4, 8, 16)
POOL_GROUP = POOL_CH // len(POOL_WINDOWS)
POOL_HIST = max(POOL_WINDOWS) - 1
EPS = 1e-6

SUBLANES = 8
LANES = 128
VMEM_LIMIT_BYTES = 60 * 1024 * 1024
WEIGHT_WINDOW = 512

SLABS = CONV_CH // LANES
assert POOL_CH == CONV_CH and POOL_GROUP % LANES == 0
SLABS_PER_GROUP = POOL_GROUP // LANES

CONV_PAD = 32
POOL_PAD = 16
CONV_OFF = CONV_PAD - CONV_HIST
POOL_OFF = POOL_PAD - POOL_HIST

FFN_ROWS = 1024
FFN_COLS = 512
FFN_COLS_F32 = 256
MIX_ROWS = 512
MIX_SUBTILES = 2
MIX_SMALL_STREAMS = 8
MAX_BLOCK_ROWS = 4 * SUBLANES


def _rmsnorm(x, g):
    ms = jnp.mean(x * x, axis=-1, keepdims=True)
    return x * lax.rsqrt(ms + EPS) * g


def _dot(a, b):
    return jnp.dot(a, b, preferred_element_type=jnp.float32)


def _ffn_kernel(x_ref, g_ref, wg_ref, wu_ref, wd_ref, fg_ref, *refs, final_norm, n_cast, emit_weights):
    n_own = 3 if emit_weights else 0
    cast_in, o_ref, xn_ref = refs[:n_cast], refs[n_cast], refs[-1]
    own_out, cast_out = refs[n_cast + 1:n_cast + 1 + n_own], refs[n_cast + 1 + n_own:-1]
    j = pl.program_id(1)

    @pl.when(j == 0)
    def _():
        x = x_ref[...]
        xn_ref[...] = _rmsnorm(x, g_ref[...]).astype(jnp.bfloat16)
        o_ref[...] = x

    wg, wu, wd = wg_ref[...], wu_ref[...], wd_ref[...]
    if emit_weights:
        wg, wu, wd = (w.astype(jnp.bfloat16) for w in (wg, wu, wd))
        for dst, w in zip(own_out, (wg, wu, wd)):
            dst[...] = w

    xn = xn_ref[...]
    gate = _dot(xn, wg)
    up = _dot(xn, wu)
    hid = (0.5 * (gate * jax.nn.sigmoid(gate)) * up).astype(jnp.bfloat16)
    o_ref[...] += _dot(hid, wd)

    for src, dst in zip(cast_in, cast_out):
        dst[...] = src[...].astype(jnp.bfloat16)

    if final_norm:
        @pl.when(j == pl.num_programs(1) - 1)
        def _():
            o_ref[...] = _rmsnorm(o_ref[...], fg_ref[...])


def _cast_spec(shape, ni, nj):
    r, c = shape
    if r % ni == 0 and c % nj == 0 and (c // nj) % LANES == 0:
        return pl.BlockSpec((r // ni, c // nj), lambda i, j: (i, j))
    assert r % nj == 0 and c % ni == 0 and (c // ni) % LANES == 0
    return pl.BlockSpec((r // nj, c // ni), lambda i, j: (j, i))


def _ffn(x, g, wg, wu, wd, fg, cast=(), *, final_norm, name):
    t, d = x.shape
    f = wg.shape[1]
    emit_weights = wg.dtype == jnp.float32
    tm = FFN_ROWS if t % FFN_ROWS == 0 else t
    tf = FFN_COLS_F32 if emit_weights else FFN_COLS
    assert t % tm == 0 and f % tf == 0
    grid = (t // tm, f // tf)
    assert not emit_weights or grid[0] == 1
    w_specs = [pl.BlockSpec((d, tf), lambda i, j: (0, j)),
               pl.BlockSpec((d, tf), lambda i, j: (0, j)),
               pl.BlockSpec((tf, d), lambda i, j: (j, 0))]
    own = [wg, wu, wd] if emit_weights else []
    cast_specs = [_cast_spec(w.shape, *grid) for w in cast]
    out = pl.pallas_call(
        functools.partial(_ffn_kernel, final_norm=final_norm, n_cast=len(cast), emit_weights=emit_weights),
        out_shape=[jax.ShapeDtypeStruct((t, d), jnp.float32)]
        + [jax.ShapeDtypeStruct(w.shape, jnp.bfloat16) for w in own + list(cast)],
        grid=grid,
        in_specs=[pl.BlockSpec((tm, d), lambda i, j: (i, 0)),
                  pl.BlockSpec((1, d), lambda i, j: (0, 0))]
        + w_specs + [pl.BlockSpec((1, d), lambda i, j: (0, 0))] + cast_specs,
        out_specs=[pl.BlockSpec((tm, d), lambda i, j: (i, 0))] + w_specs[:len(own)] + cast_specs,
        scratch_shapes=[pltpu.VMEM((tm, d), jnp.bfloat16)],
        compiler_params=pltpu.CompilerParams(
            dimension_semantics=("arbitrary", "arbitrary"),
            vmem_limit_bytes=VMEM_LIMIT_BYTES),
        name=name,
    )(x, g, wg, wu, wd, fg, *cast)
    return out[0] if len(out) == 1 else out


class _MixWeights(NamedTuple):
    norm: Any
    w_in: Any
    conv_w: Any
    conv_b: Any
    conv_norm: Any
    pool_w: Any
    pool_scale: Any
    w_out: Any


def _mix_weight_operands(w):
    def windows(v):
        k, n = v.shape
        assert n % WEIGHT_WINDOW == 0
        return [(v, pl.BlockSpec((k, WEIGHT_WINDOW), functools.partial(lambda c, *_: (0, c), c),
                                 pipeline_mode=pl.Buffered(1))) for c in range(n // WEIGHT_WINDOW)]

    pairs = []
    for name, v in w._asdict().items():
        pairs += windows(v) if name in ("w_in", "w_out") else [(v, _const_spec(v.shape))]
    return [v for v, _ in pairs], [s for _, s in pairs]


N_IN_WINDOWS = (2 * CONV_CH + POOL_CH) // WEIGHT_WINDOW
N_OUT_WINDOWS = D_MODEL // WEIGHT_WINDOW
N_MIX_WEIGHT_REFS = len(_MixWeights._fields) - 2 + N_IN_WINDOWS + N_OUT_WINDOWS


def _mix_weight_refs(refs):
    refs = list(refs)
    assert len(refs) == N_MIX_WEIGHT_REFS
    take = lambda n: tuple(refs.pop(0) for _ in range(n))
    return _MixWeights(**{name: take(N_IN_WINDOWS) if name == "w_in" else
                          take(N_OUT_WINDOWS) if name == "w_out" else refs.pop(0)
                          for name in _MixWeights._fields})


class _MixScratch(NamedTuple):
    ext_c: Any
    ext_p: Any
    y: Any
    p: Any
    hn: Any
    cp: Any


def _mix_scratch_shapes(n_streams, n_rows, n_sub=None):
    rows = n_streams * n_rows
    lead = () if n_sub is None else (n_sub,)
    new_rows = n_rows * (n_sub or 1)
    return [pltpu.VMEM((n_streams * SLABS, CONV_PAD + new_rows, LANES), jnp.float32),
            pltpu.VMEM((n_streams * SLABS, POOL_PAD + new_rows, LANES), jnp.float32),
            pltpu.VMEM(lead + (SLABS, rows, LANES), jnp.float32),
            pltpu.VMEM(lead + (SLABS, rows, LANES), jnp.float32),
            pltpu.VMEM(lead + (rows, D_MODEL), jnp.bfloat16),
            pltpu.VMEM(lead + (rows, MIX_WIDTH), jnp.bfloat16)]


def _block_rows(n_rows):
    rb = min(MAX_BLOCK_ROWS, n_rows)
    assert n_rows % rb == 0 and rb % SUBLANES == 0
    return rb, rb // SUBLANES


def _lanes(slab):
    return slice(slab * LANES, (slab + 1) * LANES)


def _mix_norm(h, w, hn_ref):
    hn_ref[...] = _rmsnorm(h, w.norm[...]).astype(jnp.bfloat16)


def _mix_inputs(w, hn_ref):
    nc = CONV_CH // WEIGHT_WINDOW
    u_conv = []
    for c in range(nc):
        a = _dot(hn_ref[...], w.w_in[c][...])
        gate = _dot(hn_ref[...], w.w_in[nc + c][...])
        u_conv.append(a * jax.nn.sigmoid(gate))
    u_pool = [_dot(hn_ref[...], w_ref[...]) for w_ref in w.w_in[2 * nc:]]
    return u_conv, u_pool


def _slab(chunks, s, rows=slice(None)):
    per = WEIGHT_WINDOW // LANES
    return chunks[s // per][rows, _lanes(s % per)]


def _causal_dwconv(ext_c, w, y_ref, n_streams, n_rows, row0):
    rb, stride = _block_rows(n_rows)
    for s in range(SLABS):
        bias = w.conv_b[s]
        taps = [w.conv_w[s, k] for k in range(CONV_K)]
        for b in range(n_streams):
            for r0 in range(0, n_rows, rb):
                acc = [bias] * stride
                for m in range(stride - 1 + CONV_K):
                    x = ext_c[b * SLABS + s, pl.ds(CONV_OFF + row0 + r0 + m, SUBLANES, stride=stride), :]
                    for j in range(stride):
                        k = m - j
                        if 0 <= k < CONV_K:
                            acc[j] = acc[j] + taps[k] * x
                for j in range(stride):
                    y_ref[s, pl.ds(b * n_rows + r0 + j, SUBLANES, stride=stride), :] = acc[j]


def _pool(ext_p, p_ref, n_streams, n_rows, row0):
    rb, stride = _block_rows(n_rows)
    for grp, window in enumerate(POOL_WINDOWS):
        for s in range(grp * SLABS_PER_GROUP, (grp + 1) * SLABS_PER_GROUP):
            for b in range(n_streams):
                for r0 in range(0, n_rows, rb):
                    for j in range(stride):
                        base = POOL_PAD + row0 + r0 + j
                        tok = ext_p[b * SLABS + s, pl.ds(base, SUBLANES, stride=stride), :]
                        tot = tok
                        for i in range(1, window):
                            tot = tot + ext_p[b * SLABS + s, pl.ds(base - i, SUBLANES, stride=stride), :]
                        p_ref[s, pl.ds(b * n_rows + r0 + j, SUBLANES, stride=stride), :] = (
                            tot * (1.0 / window) - tok)


def _mix_tail(h, w, sc, n_streams, n_rows, row0=0):
    _causal_dwconv(sc.ext_c, w, sc.y, n_streams, n_rows, row0)
    _pool(sc.ext_p, sc.p, n_streams, n_rows, row0)
    y = jnp.concatenate([sc.y[s] for s in range(SLABS)], axis=-1)
    cn = _rmsnorm(y, w.conv_norm[...])
    sc.cp[:, :CONV_CH] = (cn * jax.nn.sigmoid(cn)).astype(jnp.bfloat16)
    for grp in range(len(POOL_WINDOWS)):
        lanes = slice(grp * POOL_GROUP, (grp + 1) * POOL_GROUP)
        slabs = range(grp * SLABS_PER_GROUP, (grp + 1) * SLABS_PER_GROUP)
        p = jnp.concatenate([sc.p[s] for s in slabs], axis=-1).astype(jnp.bfloat16)
        sc.cp[:, CONV_CH + grp * POOL_GROUP:CONV_CH + (grp + 1) * POOL_GROUP] = (
            _dot(p, w.pool_w[grp]) * w.pool_scale[:, lanes]).astype(jnp.bfloat16)
    return jnp.concatenate(
        [h[:, c * WEIGHT_WINDOW:(c + 1) * WEIGHT_WINDOW] + _dot(sc.cp[...], w_ref[...])
         for c, w_ref in enumerate(w.w_out)], axis=-1)


N_MIX_SCRATCH = len(_MixScratch._fields)


def _mix_prompt_kernel(h_ref, hc_ref, hp_ref, *refs):
    w = _mix_weight_refs(refs[:N_MIX_WEIGHT_REFS])
    o_ref, ct_ref, pt_ref = refs[N_MIX_WEIGHT_REFS:-N_MIX_SCRATCH]
    sc = _MixScratch(*refs[-N_MIX_SCRATCH:])
    i = pl.program_id(0)
    tm = h_ref.shape[0]

    @pl.when(i == 0)
    def _():
        sc.ext_c[:, 0:CONV_PAD, :] = hc_ref[...]
        sc.ext_p[:, 0:POOL_PAD, :] = hp_ref[...]

    @pl.when(i > 0)
    def _():
        sc.ext_c[:, 0:CONV_PAD, :] = sc.ext_c[:, tm:tm + CONV_PAD, :]
        sc.ext_p[:, 0:POOL_PAD, :] = sc.ext_p[:, tm:tm + POOL_PAD, :]

    n_sub = sc.y.shape[0]
    sub = tm // n_sub

    def project_in(k):
        u_conv, u_pool = _mix_inputs(w, sc.hn.at[k])
        for s in range(SLABS):
            sc.ext_c[s, CONV_PAD + k * sub:CONV_PAD + (k + 1) * sub, :] = _slab(u_conv, s)
            sc.ext_p[s, POOL_PAD + k * sub:POOL_PAD + (k + 1) * sub, :] = _slab(u_pool, s)

    def mix_out(k):
        rows = slice(k * sub, (k + 1) * sub)
        sub_sc = sc._replace(y=sc.y.at[k], p=sc.p.at[k], cp=sc.cp.at[k])
        o_ref[rows, :] = _mix_tail(h_ref[rows, :], w, sub_sc, 1, sub, row0=k * sub)

    for k in range(n_sub):
        _mix_norm(h_ref[k * sub:(k + 1) * sub, :], w, sc.hn.at[k])
    project_in(0)
    for k in range(1, n_sub):
        project_in(k)
        mix_out(k - 1)
    mix_out(n_sub - 1)
    ct_ref[...] = sc.ext_c[:, tm:tm + CONV_PAD, :]
    pt_ref[...] = sc.ext_p[:, tm:tm + POOL_PAD, :]


def _mix_small_kernel(h_ref, hm_ref, sc_ref, sp_ref, *refs):
    w = _mix_weight_refs(refs[:N_MIX_WEIGHT_REFS])
    o_ref, nc_ref, np_ref, mc_ref, mp_ref = refs[N_MIX_WEIGHT_REFS:-N_MIX_SCRATCH]
    sc = _MixScratch(*refs[-N_MIX_SCRATCH:])
    n_streams = sc_ref.shape[0]
    n_rows = sc.ext_c.shape[1] - CONV_PAD

    @pl.when(pl.program_id(0) == 0)
    def _():
        _mix_norm(hm_ref[...], w, sc.hn.at[pl.ds(0, N_META)])
        m_conv, m_pool = _mix_inputs(w, sc.hn.at[pl.ds(0, N_META)])
        for s in range(SLABS):
            mc_ref[s] = _slab(m_conv, s)
            mp_ref[s] = _slab(m_pool, s)

    h = h_ref[...]
    _mix_norm(h, w, sc.hn)
    u_conv, u_pool = _mix_inputs(w, sc.hn)
    for s in range(SLABS):
        for b in range(n_streams):
            rows = slice(b * n_rows, (b + 1) * n_rows)
            sc.ext_c[b * SLABS + s, CONV_OFF:CONV_PAD, :] = sc_ref[b, :, _lanes(s)]
            sc.ext_p[b * SLABS + s, POOL_OFF:POOL_PAD, :] = sp_ref[b, :, _lanes(s)]
            sc.ext_c[b * SLABS + s, CONV_PAD:, :] = _slab(u_conv, s, rows)
            sc.ext_p[b * SLABS + s, POOL_PAD:, :] = _slab(u_pool, s, rows)

    o_ref[...] = _mix_tail(h, w, sc, n_streams, n_rows)
    for s in range(SLABS):
        for b in range(n_streams):
            nc_ref[b, :, _lanes(s)] = sc.ext_c[b * SLABS + s, n_rows + CONV_OFF:, :]
            np_ref[b, :, _lanes(s)] = sc.ext_p[b * SLABS + s, n_rows + POOL_OFF:, :]


def _const_spec(shape):
    zeros = (0,) * len(shape)
    return pl.BlockSpec(shape, lambda *_: zeros, pipeline_mode=pl.Buffered(1))


def _mix_prompt(h, hist_c, hist_p, weights):
    t, d = h.shape
    tm = MIX_ROWS
    assert t % tm == 0 and tm % MIX_SUBTILES == 0
    w_args, w_specs = _mix_weight_operands(weights)
    return pl.pallas_call(
        _mix_prompt_kernel,
        out_shape=(jax.ShapeDtypeStruct((t, d), jnp.float32),
                   jax.ShapeDtypeStruct((SLABS, CONV_PAD, LANES), jnp.float32),
                   jax.ShapeDtypeStruct((SLABS, POOL_PAD, LANES), jnp.float32)),
        grid=(t // tm,),
        in_specs=[pl.BlockSpec((tm, d), lambda i: (i, 0)),
                  _const_spec((SLABS, CONV_PAD, LANES)),
                  _const_spec((SLABS, POOL_PAD, LANES))] + w_specs,
        out_specs=(pl.BlockSpec((tm, d), lambda i: (i, 0)),
                   pl.BlockSpec((SLABS, CONV_PAD, LANES), lambda i: (0, 0, 0)),
                   pl.BlockSpec((SLABS, POOL_PAD, LANES), lambda i: (0, 0, 0))),
        scratch_shapes=_mix_scratch_shapes(1, tm // MIX_SUBTILES, MIX_SUBTILES),
        compiler_params=pltpu.CompilerParams(
            dimension_semantics=("arbitrary",),
            vmem_limit_bytes=VMEM_LIMIT_BYTES),
        name="mix_prompt",
    )(h, hist_c, hist_p, *w_args)


def _mix_small(h, state_c, state_p, weights):
    t, d = h.shape
    n_streams, _, _ = state_c.shape
    n_tok = t - N_META
    n_rows = n_tok // n_streams
    grp = MIX_SMALL_STREAMS
    tm = grp * n_rows
    assert n_rows * n_streams == n_tok and n_rows % SUBLANES == 0
    assert n_streams % grp == 0 and n_tok % N_META == 0 and tm >= N_META
    w_args, w_specs = _mix_weight_operands(weights)
    return pl.pallas_call(
        _mix_small_kernel,
        out_shape=(jax.ShapeDtypeStruct((n_tok, d), jnp.float32),
                   jax.ShapeDtypeStruct((n_streams, CONV_HIST, CONV_CH), jnp.float32),
                   jax.ShapeDtypeStruct((n_streams, POOL_HIST, POOL_CH), jnp.float32),
                   jax.ShapeDtypeStruct((SLABS, N_META, LANES), jnp.float32),
                   jax.ShapeDtypeStruct((SLABS, N_META, LANES), jnp.float32)),
        grid=(n_streams // grp,),
        in_specs=[pl.BlockSpec((tm, d), lambda i: (i, 0)),
                  pl.BlockSpec((N_META, d), lambda i: (n_tok // N_META, 0), pipeline_mode=pl.Buffered(1)),
                  pl.BlockSpec((grp, CONV_HIST, CONV_CH), lambda i: (i, 0, 0)),
                  pl.BlockSpec((grp, POOL_HIST, POOL_CH), lambda i: (i, 0, 0))]
        + w_specs,
        out_specs=(pl.BlockSpec((tm, d), lambda i: (i, 0)),
                   pl.BlockSpec((grp, CONV_HIST, CONV_CH), lambda i: (i, 0, 0)),
                   pl.BlockSpec((grp, POOL_HIST, POOL_CH), lambda i: (i, 0, 0)),
                   pl.BlockSpec((SLABS, N_META, LANES), lambda i: (0, 0, 0)),
                   pl.BlockSpec((SLABS, N_META, LANES), lambda i: (0, 0, 0))),
        scratch_shapes=_mix_scratch_shapes(grp, n_rows),
        compiler_params=pltpu.CompilerParams(
            dimension_semantics=("arbitrary",),
            vmem_limit_bytes=VMEM_LIMIT_BYTES),
        name="mix_small",
    )(h, h, state_c, state_p, *w_args)


def _to_slabs(v):
    return v.reshape(v.shape[0], SLABS, LANES).transpose(1, 0, 2)


def _from_slabs(v):
    return v.transpose(1, 0, 2).reshape(v.shape[1], SLABS * LANES)


def kernel(x_prompt, x_sample, state_conv, state_pool, meta_tokens, ffn1_norm, ffn1_w_gate, ffn1_w_up, ffn1_w_down, mix_norm, w_in, conv_w, conv_b, conv_norm, pool_w, pool_scale, w_out, ffn2_norm, ffn2_w_gate, ffn2_w_up, ffn2_w_down, final_norm):
    depth = ffn1_norm.shape[0]
    b_p, seq, d = x_prompt.shape
    b_s, seq_s, _ = x_sample.shape
    assert depth == 1 and b_p == 1
    bf = jnp.bfloat16

    xp = x_prompt.reshape(seq, d)
    xs = jnp.concatenate([x_sample.reshape(b_s * seq_s, d), meta_tokens.astype(x_sample.dtype)], axis=0)

    row = lambda v: v.reshape(1, -1)
    fg = row(final_norm)
    hs, wg1, wu1, wd1 = _ffn(xs, row(ffn1_norm[0]), ffn1_w_gate[0], ffn1_w_up[0], ffn1_w_down[0], fg,
                             final_norm=False, name="ffn1_small")
    hp, wg2, wu2, wd2 = _ffn(xp, row(ffn1_norm[0]), wg1, wu1, wd1, fg,
                             cast=(ffn2_w_gate[0], ffn2_w_up[0], ffn2_w_down[0]),
                             final_norm=False, name="ffn1_prompt")
    ffn2 = (row(ffn2_norm[0]), wg2, wu2, wd2, fg)
    mixw = _MixWeights(
        norm=row(mix_norm[0]),
        w_in=w_in[0].astype(bf),
        conv_w=jnp.broadcast_to(_to_slabs(conv_w[0])[:, :, None, :], (SLABS, CONV_K, SUBLANES, LANES)),
        conv_b=jnp.broadcast_to(_to_slabs(row(conv_b[0])), (SLABS, SUBLANES, LANES)),
        conv_norm=row(conv_norm[0]),
        pool_w=pool_w[0].astype(bf),
        pool_scale=row(pool_scale[0]),
        w_out=w_out[0].astype(bf))

    hs, new_conv_s, new_pool_s, meta_c, meta_p = _mix_small(hs, state_conv[0], state_pool[0], mixw)
    hist_c = jnp.concatenate([jnp.zeros((SLABS, CONV_PAD - N_META, LANES), jnp.float32), meta_c], axis=1)
    hp, conv_tail, pool_tail = _mix_prompt(hp, hist_c, meta_p, mixw)

    ys = _ffn(hs, *ffn2, final_norm=True, name="ffn2_small")
    yp = _ffn(hp, *ffn2, final_norm=True, name="ffn2_prompt")

    y_prompt = yp.reshape(b_p, seq, d)
    y_sample = ys.reshape(b_s, seq_s, d)
    new_conv_prompt = _from_slabs(conv_tail[:, CONV_OFF:, :])[None, None]
    new_pool_prompt = _from_slabs(pool_tail[:, POOL_OFF:, :])[None, None]
    return (y_prompt, y_sample, new_conv_prompt, new_pool_prompt, new_conv_s[None], new_pool_s[None])
```

```python
import functools
from typing import Any, NamedTuple

import jax
import jax.numpy as jnp
from jax import lax
from jax.experimental import pallas as pl
from jax.experimental.pallas import tpu as pltpu

D_MODEL = 2048
N_META = 16
CONV_CH = 1024
POOL_CH = 1024
MIX_WIDTH = CONV_CH + POOL_CH
CONV_K = 31
CONV_HIST = CONV_K - 1
POOL_WINDOWS = (2, 4, 8, 16)
POOL_GROUP = POOL_CH // len(POOL_WINDOWS)
POOL_HIST = max(POOL_WINDOWS) - 1
EPS = 1e-6

SUBLANES = 8
LANES = 128
VMEM_LIMIT_BYTES = 60 * 1024 * 1024
WEIGHT_WINDOW = 512

SLABS = CONV_CH // LANES
assert POOL_CH == CONV_CH and POOL_GROUP % LANES == 0
SLABS_PER_GROUP = POOL_GROUP // LANES

CONV_PAD = 32
POOL_PAD = 16
CONV_OFF = CONV_PAD - CONV_HIST
POOL_OFF = POOL_PAD - POOL_HIST

FFN_ROWS = 1024
FFN_COLS = 512
FFN_COLS_F32 = 512
MIX_ROWS = 512
MIX_SMALL_STREAMS = 8
MAX_BLOCK_ROWS = 4 * SUBLANES


def _rmsnorm(x, g):
    ms = jnp.mean(x * x, axis=-1, keepdims=True)
    return x * lax.rsqrt(ms + EPS) * g


def _dot(a, b):
    return jnp.dot(a, b, preferred_element_type=jnp.float32)


def _ffn_kernel(x_ref, g_ref, wg_ref, wu_ref, wd_ref, fg_ref, *refs, final_norm, n_cast, emit_weights):
    n_own = 3 if emit_weights else 0
    cast_in, o_ref, xn_ref = refs[:n_cast], refs[n_cast], refs[-1]
    own_out, cast_out = refs[n_cast + 1:n_cast + 1 + n_own], refs[n_cast + 1 + n_own:-1]
    j = pl.program_id(1)

    @pl.when(j == 0)
    def _():
        x = x_ref[...]
        xn_ref[...] = _rmsnorm(x, g_ref[...]).astype(jnp.bfloat16)
        o_ref[...] = x

    wg, wu, wd = wg_ref[...], wu_ref[...], wd_ref[...]
    if emit_weights:
        wg, wu, wd = (w.astype(jnp.bfloat16) for w in (wg, wu, wd))
        for dst, w in zip(own_out, (wg, wu, wd)):
            dst[...] = w

    xn = xn_ref[...]
    gate = _dot(xn, wg)
    up = _dot(xn, wu)
    hid = (0.5 * (gate * jax.nn.sigmoid(gate)) * up).astype(jnp.bfloat16)
    o_ref[...] += _dot(hid, wd)

    for src, dst in zip(cast_in, cast_out):
        dst[...] = src[...].astype(jnp.bfloat16)

    if final_norm:
        @pl.when(j == pl.num_programs(1) - 1)
        def _():
            o_ref[...] = _rmsnorm(o_ref[...], fg_ref[...])


def _cast_spec(shape, ni, nj):
    r, c = shape
    if r % ni == 0 and c % nj == 0 and (c // nj) % LANES == 0:
        return pl.BlockSpec((r // ni, c // nj), lambda i, j: (i, j))
    assert r % nj == 0 and c % ni == 0 and (c // ni) % LANES == 0
    return pl.BlockSpec((r // nj, c // ni), lambda i, j: (j, i))


def _ffn(x, g, wg, wu, wd, fg, cast=(), *, final_norm, name):
    t, d = x.shape
    f = wg.shape[1]
    emit_weights = wg.dtype == jnp.float32
    tm = FFN_ROWS if t % FFN_ROWS == 0 else t
    tf = FFN_COLS_F32 if emit_weights else FFN_COLS
    assert t % tm == 0 and f % tf == 0
    grid = (t // tm, f // tf)
    assert not emit_weights or grid[0] == 1
    w_specs = [pl.BlockSpec((d, tf), lambda i, j: (0, j)),
               pl.BlockSpec((d, tf), lambda i, j: (0, j)),
               pl.BlockSpec((tf, d), lambda i, j: (j, 0))]
    own = [wg, wu, wd] if emit_weights else []
    cast_specs = [_cast_spec(w.shape, *grid) for w in cast]
    out = pl.pallas_call(
        functools.partial(_ffn_kernel, final_norm=final_norm, n_cast=len(cast), emit_weights=emit_weights),
        out_shape=[jax.ShapeDtypeStruct((t, d), jnp.float32)]
        + [jax.ShapeDtypeStruct(w.shape, jnp.bfloat16) for w in own + list(cast)],
        grid=grid,
        in_specs=[pl.BlockSpec((tm, d), lambda i, j: (i, 0)),
                  pl.BlockSpec((1, d), lambda i, j: (0, 0))]
        + w_specs + [pl.BlockSpec((1, d), lambda i, j: (0, 0))] + cast_specs,
        out_specs=[pl.BlockSpec((tm, d), lambda i, j: (i, 0))] + w_specs[:len(own)] + cast_specs,
        scratch_shapes=[pltpu.VMEM((tm, d), jnp.bfloat16)],
        compiler_params=pltpu.CompilerParams(
            dimension_semantics=("arbitrary", "arbitrary"),
            vmem_limit_bytes=VMEM_LIMIT_BYTES),
        name=name,
    )(x, g, wg, wu, wd, fg, *cast)
    return out[0] if len(out) == 1 else out


class _MixWeights(NamedTuple):
    norm: Any
    w_in: Any
    conv_w: Any
    conv_b: Any
    conv_norm: Any
    pool_w: Any
    pool_scale: Any
    w_out: Any


def _mix_weight_operands(w):
    def windows(v):
        k, n = v.shape
        assert n % WEIGHT_WINDOW == 0
        return [(v, pl.BlockSpec((k, WEIGHT_WINDOW), functools.partial(lambda c, *_: (0, c), c),
                                 pipeline_mode=pl.Buffered(1))) for c in range(n // WEIGHT_WINDOW)]

    pairs = []
    for name, v in w._asdict().items():
        pairs += windows(v) if name in ("w_in", "w_out") else [(v, _const_spec(v.shape))]
    return [v for v, _ in pairs], [s for _, s in pairs]


N_IN_WINDOWS = (2 * CONV_CH + POOL_CH) // WEIGHT_WINDOW
N_OUT_WINDOWS = D_MODEL // WEIGHT_WINDOW
N_MIX_WEIGHT_REFS = len(_MixWeights._fields) - 2 + N_IN_WINDOWS + N_OUT_WINDOWS


def _mix_weight_refs(refs):
    refs = list(refs)
    assert len(refs) == N_MIX_WEIGHT_REFS
    take = lambda n: tuple(refs.pop(0) for _ in range(n))
    return _MixWeights(**{name: take(N_IN_WINDOWS) if name == "w_in" else
                          take(N_OUT_WINDOWS) if name == "w_out" else refs.pop(0)
                          for name in _MixWeights._fields})


class _MixScratch(NamedTuple):
    ext_c: Any
    ext_p: Any
    y: Any
    p: Any
    hn: Any
    cp: Any


def _mix_scratch_shapes(n_streams, n_rows):
    rows = n_streams * n_rows
    return [pltpu.VMEM((n_streams * SLABS, CONV_PAD + n_rows, LANES), jnp.float32),
            pltpu.VMEM((n_streams * SLABS, POOL_PAD + n_rows, LANES), jnp.float32),
            pltpu.VMEM((SLABS, rows, LANES), jnp.float32),
            pltpu.VMEM((SLABS, rows, LANES), jnp.float32),
            pltpu.VMEM((rows, D_MODEL), jnp.bfloat16),
            pltpu.VMEM((rows, MIX_WIDTH), jnp.bfloat16)]


def _block_rows(n_rows):
    rb = min(MAX_BLOCK_ROWS, n_rows)
    assert n_rows % rb == 0 and rb % SUBLANES == 0
    return rb, rb // SUBLANES


def _lanes(slab):
    return slice(slab * LANES, (slab + 1) * LANES)


def _mix_inputs(h, w, hn_ref):
    hn_ref[...] = _rmsnorm(h, w.norm[...]).astype(jnp.bfloat16)
    nc = CONV_CH // WEIGHT_WINDOW
    u_conv = []
    for c in range(nc):
        a = _dot(hn_ref[...], w.w_in[c][...])
        gate = _dot(hn_ref[...], w.w_in[nc + c][...])
        u_conv.append(a * jax.nn.sigmoid(gate))
    u_pool = [_dot(hn_ref[...], w_ref[...]) for w_ref in w.w_in[2 * nc:]]
    return u_conv, u_pool


def _slab(chunks, s, rows=slice(None)):
    per = WEIGHT_WINDOW // LANES
    return chunks[s // per][rows, _lanes(s % per)]


def _causal_dwconv(ext_c, w, y_ref, n_streams, n_rows):
    rb, stride = _block_rows(n_rows)
    for s in range(SLABS):
        bias = w.conv_b[s]
        taps = [w.conv_w[s, k] for k in range(CONV_K)]
        for b in range(n_streams):
            for r0 in range(0, n_rows, rb):
                acc = [bias] * stride
                for m in range(stride - 1 + CONV_K):
                    x = ext_c[b * SLABS + s, pl.ds(CONV_OFF + r0 + m, SUBLANES, stride=stride), :]
                    for j in range(stride):
                        k = m - j
                        if 0 <= k < CONV_K:
                            acc[j] = acc[j] + taps[k] * x
                for j in range(stride):
                    y_ref[s, pl.ds(b * n_rows + r0 + j, SUBLANES, stride=stride), :] = acc[j]


def _pool(ext_p, p_ref, n_streams, n_rows):
    rb, stride = _block_rows(n_rows)
    for grp, window in enumerate(POOL_WINDOWS):
        for s in range(grp * SLABS_PER_GROUP, (grp + 1) * SLABS_PER_GROUP):
            for b in range(n_streams):
                for r0 in range(0, n_rows, rb):
                    for j in range(stride):
                        base = POOL_PAD + r0 + j
                        tok = ext_p[b * SLABS + s, pl.ds(base, SUBLANES, stride=stride), :]
                        tot = tok
                        for i in range(1, window):
                            tot = tot + ext_p[b * SLABS + s, pl.ds(base - i, SUBLANES, stride=stride), :]
                        p_ref[s, pl.ds(b * n_rows + r0 + j, SUBLANES, stride=stride), :] = (
                            tot * (1.0 / window) - tok)


def _mix_tail(h, w, sc, n_streams, n_rows):
    _causal_dwconv(sc.ext_c, w, sc.y, n_streams, n_rows)
    _pool(sc.ext_p, sc.p, n_streams, n_rows)
    y = jnp.concatenate([sc.y[s] for s in range(SLABS)], axis=-1)
    cn = _rmsnorm(y, w.conv_norm[...])
    sc.cp[:, :CONV_CH] = (cn * jax.nn.sigmoid(cn)).astype(jnp.bfloat16)
    for grp in range(len(POOL_WINDOWS)):
        lanes = slice(grp * POOL_GROUP, (grp + 1) * POOL_GROUP)
        slabs = range(grp * SLABS_PER_GROUP, (grp + 1) * SLABS_PER_GROUP)
        p = jnp.concatenate([sc.p[s] for s in slabs], axis=-1).astype(jnp.bfloat16)
        sc.cp[:, CONV_CH + grp * POOL_GROUP:CONV_CH + (grp + 1) * POOL_GROUP] = (
            _dot(p, w.pool_w[grp]) * w.pool_scale[:, lanes]).astype(jnp.bfloat16)
    return jnp.concatenate(
        [h[:, c * WEIGHT_WINDOW:(c + 1) * WEIGHT_WINDOW] + _dot(sc.cp[...], w_ref[...])
         for c, w_ref in enumerate(w.w_out)], axis=-1)


N_MIX_SCRATCH = len(_MixScratch._fields)


def _mix_prompt_kernel(h_ref, hc_ref, hp_ref, *refs):
    w = _mix_weight_refs(refs[:N_MIX_WEIGHT_REFS])
    o_ref, ct_ref, pt_ref = refs[N_MIX_WEIGHT_REFS:-N_MIX_SCRATCH]
    sc = _MixScratch(*refs[-N_MIX_SCRATCH:])
    i = pl.program_id(0)
    tm = h_ref.shape[0]

    @pl.when(i == 0)
    def _():
        sc.ext_c[:, 0:CONV_PAD, :] = hc_ref[...]
        sc.ext_p[:, 0:POOL_PAD, :] = hp_ref[...]

    @pl.when(i > 0)
    def _():
        sc.ext_c[:, 0:CONV_PAD, :] = sc.ext_c[:, tm:tm + CONV_PAD, :]
        sc.ext_p[:, 0:POOL_PAD, :] = sc.ext_p[:, tm:tm + POOL_PAD, :]

    h = h_ref[...]
    u_conv, u_pool = _mix_inputs(h, w, sc.hn)
    for s in range(SLABS):
        sc.ext_c[s, CONV_PAD:, :] = _slab(u_conv, s)
        sc.ext_p[s, POOL_PAD:, :] = _slab(u_pool, s)
    o_ref[...] = _mix_tail(h, w, sc, 1, tm)
    ct_ref[...] = sc.ext_c[:, tm:tm + CONV_PAD, :]
    pt_ref[...] = sc.ext_p[:, tm:tm + POOL_PAD, :]


def _mix_small_kernel(h_ref, hm_ref, sc_ref, sp_ref, *refs):
    w = _mix_weight_refs(refs[:N_MIX_WEIGHT_REFS])
    o_ref, nc_ref, np_ref, mc_ref, mp_ref = refs[N_MIX_WEIGHT_REFS:-N_MIX_SCRATCH]
    sc = _MixScratch(*refs[-N_MIX_SCRATCH:])
    n_streams = sc_ref.shape[0]
    n_rows = sc.ext_c.shape[1] - CONV_PAD

    @pl.when(pl.program_id(0) == 0)
    def _():
        m_conv, m_pool = _mix_inputs(hm_ref[...], w, sc.hn.at[pl.ds(0, N_META)])
        for s in range(SLABS):
            mc_ref[s] = _slab(m_conv, s)
            mp_ref[s] = _slab(m_pool, s)

    h = h_ref[...]
    u_conv, u_pool = _mix_inputs(h, w, sc.hn)
    for s in range(SLABS):
        for b in range(n_streams):
            rows = slice(b * n_rows, (b + 1) * n_rows)
            sc.ext_c[b * SLABS + s, CONV_OFF:CONV_PAD, :] = sc_ref[b, :, _lanes(s)]
            sc.ext_p[b * SLABS + s, POOL_OFF:POOL_PAD, :] = sp_ref[b, :, _lanes(s)]
            sc.ext_c[b * SLABS + s, CONV_PAD:, :] = _slab(u_conv, s, rows)
            sc.ext_p[b * SLABS + s, POOL_PAD:, :] = _slab(u_pool, s, rows)

    o_ref[...] = _mix_tail(h, w, sc, n_streams, n_rows)
    for s in range(SLABS):
        for b in range(n_streams):
            nc_ref[b, :, _lanes(s)] = sc.ext_c[b * SLABS + s, n_rows + CONV_OFF:, :]
            np_ref[b, :, _lanes(s)] = sc.ext_p[b * SLABS + s, n_rows + POOL_OFF:, :]


def _const_spec(shape):
    zeros = (0,) * len(shape)
    return pl.BlockSpec(shape, lambda *_: zeros, pipeline_mode=pl.Buffered(1))


def _mix_prompt(h, hist_c, hist_p, weights):
    t, d = h.shape
    tm = MIX_ROWS
    assert t % tm == 0
    w_args, w_specs = _mix_weight_operands(weights)
    return pl.pallas_call(
        _mix_prompt_kernel,
        out_shape=(jax.ShapeDtypeStruct((t, d), jnp.float32),
                   jax.ShapeDtypeStruct((SLABS, CONV_PAD, LANES), jnp.float32),
                   jax.ShapeDtypeStruct((SLABS, POOL_PAD, LANES), jnp.float32)),
        grid=(t // tm,),
        in_specs=[pl.BlockSpec((tm, d), lambda i: (i, 0)),
                  _const_spec((SLABS, CONV_PAD, LANES)),
                  _const_spec((SLABS, POOL_PAD, LANES))] + w_specs,
        out_specs=(pl.BlockSpec((tm, d), lambda i: (i, 0)),
                   pl.BlockSpec((SLABS, CONV_PAD, LANES), lambda i: (0, 0, 0)),
                   pl.BlockSpec((SLABS, POOL_PAD, LANES), lambda i: (0, 0, 0))),
        scratch_shapes=_mix_scratch_shapes(1, tm),
        compiler_params=pltpu.CompilerParams(
            dimension_semantics=("arbitrary",),
            vmem_limit_bytes=VMEM_LIMIT_BYTES),
        name="mix_prompt",
    )(h, hist_c, hist_p, *w_args)


def _mix_small(h, state_c, state_p, weights):
    t, d = h.shape
    n_streams, _, _ = state_c.shape
    n_tok = t - N_META
    n_rows = n_tok // n_streams
    grp = MIX_SMALL_STREAMS
    tm = grp * n_rows
    assert n_rows * n_streams == n_tok and n_rows % SUBLANES == 0
    assert n_streams % grp == 0 and n_tok % N_META == 0 and tm >= N_META
    w_args, w_specs = _mix_weight_operands(weights)
    return pl.pallas_call(
        _mix_small_kernel,
        out_shape=(jax.ShapeDtypeStruct((n_tok, d), jnp.float32),
                   jax.ShapeDtypeStruct((n_streams, CONV_HIST, CONV_CH), jnp.float32),
                   jax.ShapeDtypeStruct((n_streams, POOL_HIST, POOL_CH), jnp.float32),
                   jax.ShapeDtypeStruct((SLABS, N_META, LANES), jnp.float32),
                   jax.ShapeDtypeStruct((SLABS, N_META, LANES), jnp.float32)),
        grid=(n_streams // grp,),
        in_specs=[pl.BlockSpec((tm, d), lambda i: (i, 0)),
                  pl.BlockSpec((N_META, d), lambda i: (n_tok // N_META, 0), pipeline_mode=pl.Buffered(1)),
                  pl.BlockSpec((grp, CONV_HIST, CONV_CH), lambda i: (i, 0, 0)),
                  pl.BlockSpec((grp, POOL_HIST, POOL_CH), lambda i: (i, 0, 0))]
        + w_specs,
        out_specs=(pl.BlockSpec((tm, d), lambda i: (i, 0)),
                   pl.BlockSpec((grp, CONV_HIST, CONV_CH), lambda i: (i, 0, 0)),
                   pl.BlockSpec((grp, POOL_HIST, POOL_CH), lambda i: (i, 0, 0)),
                   pl.BlockSpec((SLABS, N_META, LANES), lambda i: (0, 0, 0)),
                   pl.BlockSpec((SLABS, N_META, LANES), lambda i: (0, 0, 0))),
        scratch_shapes=_mix_scratch_shapes(grp, n_rows),
        compiler_params=pltpu.CompilerParams(
            dimension_semantics=("arbitrary",),
            vmem_limit_bytes=VMEM_LIMIT_BYTES),
        name="mix_small",
    )(h, h, state_c, state_p, *w_args)


def _to_slabs(v):
    return v.reshape(v.shape[0], SLABS, LANES).transpose(1, 0, 2)


def _from_slabs(v):
    return v.transpose(1, 0, 2).reshape(v.shape[1], SLABS * LANES)


def kernel(x_prompt, x_sample, state_conv, state_pool, meta_tokens, ffn1_norm, ffn1_w_gate, ffn1_w_up, ffn1_w_down, mix_norm, w_in, conv_w, conv_b, conv_norm, pool_w, pool_scale, w_out, ffn2_norm, ffn2_w_gate, ffn2_w_up, ffn2_w_down, final_norm):
    depth = ffn1_norm.shape[0]
    b_p, seq, d = x_prompt.shape
    b_s, seq_s, _ = x_sample.shape
    assert depth == 1 and b_p == 1
    bf = jnp.bfloat16

    xp = x_prompt.reshape(seq, d)
    xs = jnp.concatenate([x_sample.reshape(b_s * seq_s, d), meta_tokens.astype(x_sample.dtype)], axis=0)

    row = lambda v: v.reshape(1, -1)
    fg = row(final_norm)
    hs, wg1, wu1, wd1 = _ffn(xs, row(ffn1_norm[0]), ffn1_w_gate[0], ffn1_w_up[0], ffn1_w_down[0], fg,
                             final_norm=False, name="ffn1_small")
    hp, wg2, wu2, wd2 = _ffn(xp, row(ffn1_norm[0]), wg1, wu1, wd1, fg,
                             cast=(ffn2_w_gate[0], ffn2_w_up[0], ffn2_w_down[0]),
                             final_norm=False, name="ffn1_prompt")
    ffn2 = (row(ffn2_norm[0]), wg2, wu2, wd2, fg)
    mixw = _MixWeights(
        norm=row(mix_norm[0]),
        w_in=w_in[0].astype(bf),
        conv_w=jnp.broadcast_to(_to_slabs(conv_w[0])[:, :, None, :], (SLABS, CONV_K, SUBLANES, LANES)),
        conv_b=jnp.broadcast_to(_to_slabs(row(conv_b[0])), (SLABS, SUBLANES, LANES)),
        conv_norm=row(conv_norm[0]),
        pool_w=pool_w[0].astype(bf),
        pool_scale=row(pool_scale[0]),
        w_out=w_out[0].astype(bf))

    hs, new_conv_s, new_pool_s, meta_c, meta_p = _mix_small(hs, state_conv[0], state_pool[0], mixw)
    hist_c = jnp.concatenate([jnp.zeros((SLABS, CONV_PAD - N_META, LANES), jnp.float32), meta_c], axis=1)
    hp, conv_tail, pool_tail = _mix_prompt(hp, hist_c, meta_p, mixw)

    ys = _ffn(hs, *ffn2, final_norm=True, name="ffn2_small")
    yp = _ffn(hp, *ffn2, final_norm=True, name="ffn2_prompt")

    y_prompt = yp.reshape(b_p, seq, d)
    y_sample = ys.reshape(b_s, seq_s, d)
    new_conv_prompt = _from_slabs(conv_tail[:, CONV_OFF:, :])[None, None]
    new_pool_prompt = _from_slabs(pool_tail[:, POOL_OFF:, :])[None, None]
    return (y_prompt, y_sample, new_conv_prompt, new_pool_prompt, new_conv_s[None], new_pool_s[None])
```

```python
import functools
from typing import Any, NamedTuple

import jax
import jax.numpy as jnp
from jax import lax
from jax.experimental import pallas as pl
from jax.experimental.pallas import tpu as pltpu

D_MODEL = 2048
N_META = 16
CONV_CH = 1024
POOL_CH = 1024
MIX_WIDTH = CONV_CH + POOL_CH
CONV_K = 31
CONV_HIST = CONV_K - 1
POOL_WINDOWS = (2, 4, 8, 16)
POOL_GROUP = POOL_CH // len(POOL_WINDOWS)
POOL_HIST = max(POOL_WINDOWS) - 1
EPS = 1e-6

SUBLANES = 8
BF16_SUBLANES = 16
LANES = 128
VMEM_LIMIT_BYTES = 60 * 1024 * 1024
WEIGHT_WINDOW = 512

SLABS = CONV_CH // LANES
assert POOL_CH == CONV_CH and POOL_GROUP % LANES == 0
SLABS_PER_GROUP = POOL_GROUP // LANES

CONV_PAD = 32
POOL_PAD = 16
CONV_OFF = CONV_PAD - CONV_HIST
POOL_OFF = POOL_PAD - POOL_HIST

FFN_ROWS = 1024
FFN_COLS = 512
MIX_ROWS = 512
MIX_SMALL_STREAMS = 16
MAX_BLOCK_ROWS = 4 * SUBLANES


def _rmsnorm(x, g):
    ms = jnp.mean(x * x, axis=-1, keepdims=True)
    return x * lax.rsqrt(ms + EPS) * g


def _dot(a, b):
    return jnp.dot(a, b, preferred_element_type=jnp.float32)


def _ffn_kernel(x_hbm, g_ref, wg_ref, wu_ref, wd_ref, fg_ref, *refs, final_norm, n_cast, emit_weights):
    n_own = 3 if emit_weights else 0
    cast_in, o_ref = refs[:n_cast], refs[n_cast]
    xn_ref, x_buf, x_sem = refs[-3:]
    own_out, cast_out = refs[n_cast + 1:n_cast + 1 + n_own], refs[n_cast + 1 + n_own:-3]
    i, j = pl.program_id(0), pl.program_id(1)
    tm = x_buf.shape[0]

    def x_copy(tile):
        return pltpu.make_async_copy(x_hbm.at[pl.ds(tile * tm, tm), :], x_buf, x_sem)

    @pl.when(j == 0)
    def _():
        @pl.when(i == 0)
        def _():
            x_copy(0).start()

        x_copy(i).wait()
        x = x_buf[...]
        xn_ref[...] = _rmsnorm(x, g_ref[...]).astype(jnp.bfloat16)
        o_ref[...] = x

        @pl.when(i + 1 < pl.num_programs(0))
        def _():
            x_copy(i + 1).start()

    wg, wu, wd = wg_ref[...], wu_ref[...], wd_ref[...]
    if emit_weights:
        wg, wu, wd = (w.astype(jnp.bfloat16) for w in (wg, wu, wd))
        for dst, w in zip(own_out, (wg, wu, wd)):
            dst[...] = w

    xn = xn_ref[...]
    gate = _dot(xn, wg)
    up = _dot(xn, wu)
    hid = (0.5 * (gate * jax.nn.sigmoid(gate)) * up).astype(jnp.bfloat16)
    o_ref[...] += _dot(hid, wd)

    for src, dst in zip(cast_in, cast_out):
        dst[...] = src[...].astype(jnp.bfloat16)

    if final_norm:
        @pl.when(j == pl.num_programs(1) - 1)
        def _():
            o_ref[...] = _rmsnorm(o_ref[...], fg_ref[...])


def _cast_spec(shape, ni, nj):
    r, c = shape
    if r % ni == 0 and c % nj == 0 and (c // nj) % LANES == 0:
        return pl.BlockSpec((r // ni, c // nj), lambda i, j: (i, j))
    if r % nj == 0 and c % ni == 0 and (c // ni) % LANES == 0:
        return pl.BlockSpec((r // nj, c // ni), lambda i, j: (j, i))
    nb = max(n for n in range(1, nj + 1) if c % n == 0 and (c // n) % LANES == 0)
    assert r % ni == 0 and (r // ni) % BF16_SUBLANES == 0
    return pl.BlockSpec((r // ni, c // nb), lambda i, j: (i, jnp.minimum(j, nb - 1)))


def _ffn(x, g, wg, wu, wd, fg, cast=(), *, final_norm, name):
    t, d = x.shape
    f = wg.shape[1]
    emit_weights = wg.dtype == jnp.float32
    tm = FFN_ROWS if t % FFN_ROWS == 0 else t
    tf = FFN_COLS
    assert t % tm == 0 and f % tf == 0
    grid = (t // tm, f // tf)
    assert not emit_weights or grid[0] == 1
    w_specs = [pl.BlockSpec((d, tf), lambda i, j: (0, j)),
               pl.BlockSpec((d, tf), lambda i, j: (0, j)),
               pl.BlockSpec((tf, d), lambda i, j: (j, 0))]
    own = [wg, wu, wd] if emit_weights else []
    cast_specs = [_cast_spec(w.shape, *grid) for w in cast]
    out = pl.pallas_call(
        functools.partial(_ffn_kernel, final_norm=final_norm, n_cast=len(cast), emit_weights=emit_weights),
        out_shape=[jax.ShapeDtypeStruct((t, d), jnp.float32)]
        + [jax.ShapeDtypeStruct(w.shape, jnp.bfloat16) for w in own + list(cast)],
        grid=grid,
        in_specs=[pl.BlockSpec(memory_space=pl.ANY),
                  pl.BlockSpec((1, d), lambda i, j: (0, 0))]
        + w_specs + [pl.BlockSpec((1, d), lambda i, j: (0, 0))] + cast_specs,
        out_specs=[pl.BlockSpec((tm, d), lambda i, j: (i, 0))] + w_specs[:len(own)] + cast_specs,
        scratch_shapes=[pltpu.VMEM((tm, d), jnp.bfloat16),
                        pltpu.VMEM((tm, d), jnp.float32),
                        pltpu.SemaphoreType.DMA(())],
        compiler_params=pltpu.CompilerParams(
            dimension_semantics=("arbitrary", "arbitrary"),
            vmem_limit_bytes=VMEM_LIMIT_BYTES),
        name=name,
    )(x, g, wg, wu, wd, fg, *cast)
    return out[0] if len(out) == 1 else out


class _MixWeights(NamedTuple):
    norm: Any
    w_in: Any
    conv_w: Any
    conv_b: Any
    conv_norm: Any
    pool_w: Any
    pool_scale: Any
    w_out: Any


def _mix_weight_operands(w):
    def windows(v):
        k, n = v.shape
        assert n % WEIGHT_WINDOW == 0
        return [(v, pl.BlockSpec((k, WEIGHT_WINDOW), functools.partial(lambda c, *_: (0, c), c),
                                 pipeline_mode=pl.Buffered(1))) for c in range(n // WEIGHT_WINDOW)]

    pairs = []
    for name, v in w._asdict().items():
        pairs += windows(v) if name in ("w_in", "w_out") else [(v, _const_spec(v.shape))]
    return [v for v, _ in pairs], [s for _, s in pairs]


N_IN_WINDOWS = (2 * CONV_CH + POOL_CH) // WEIGHT_WINDOW
N_OUT_WINDOWS = D_MODEL // WEIGHT_WINDOW
N_MIX_WEIGHT_REFS = len(_MixWeights._fields) - 2 + N_IN_WINDOWS + N_OUT_WINDOWS


def _mix_weight_refs(refs):
    refs = list(refs)
    assert len(refs) == N_MIX_WEIGHT_REFS
    take = lambda n: tuple(refs.pop(0) for _ in range(n))
    return _MixWeights(**{name: take(N_IN_WINDOWS) if name == "w_in" else
                          take(N_OUT_WINDOWS) if name == "w_out" else refs.pop(0)
                          for name in _MixWeights._fields})


class _MixScratch(NamedTuple):
    ext_c: Any
    ext_p: Any
    y: Any
    p: Any
    hn: Any
    cp: Any


def _mix_scratch_shapes(n_streams, n_rows):
    rows = n_streams * n_rows
    return [pltpu.VMEM((n_streams * SLABS, CONV_PAD + n_rows, LANES), jnp.float32),
            pltpu.VMEM((n_streams * SLABS, POOL_PAD + n_rows, LANES), jnp.float32),
            pltpu.VMEM((SLABS, rows, LANES), jnp.float32),
            pltpu.VMEM((SLABS, rows, LANES), jnp.float32),
            pltpu.VMEM((rows, D_MODEL), jnp.bfloat16),
            pltpu.VMEM((rows, MIX_WIDTH), jnp.bfloat16)]


def _block_rows(n_rows):
    rb = min(MAX_BLOCK_ROWS, n_rows)
    assert n_rows % rb == 0 and rb % SUBLANES == 0
    return rb, rb // SUBLANES


def _lanes(slab):
    return slice(slab * LANES, (slab + 1) * LANES)


def _mix_inputs(h, w, hn_ref):
    hn_ref[...] = _rmsnorm(h, w.norm[...]).astype(jnp.bfloat16)
    nc = CONV_CH // WEIGHT_WINDOW
    u_conv = []
    for c in range(nc):
        a = _dot(hn_ref[...], w.w_in[c][...])
        gate = _dot(hn_ref[...], w.w_in[nc + c][...])
        u_conv.append(a * jax.nn.sigmoid(gate))
    u_pool = [_dot(hn_ref[...], w_ref[...]) for w_ref in w.w_in[2 * nc:]]
    return u_conv, u_pool


def _slab(chunks, s, rows=slice(None)):
    per = WEIGHT_WINDOW // LANES
    return chunks[s // per][rows, _lanes(s % per)]


def _causal_dwconv(ext_c, w, y_ref, n_streams, n_rows):
    rb, stride = _block_rows(n_rows)
    for s in range(SLABS):
        bias = w.conv_b[s]
        taps = [w.conv_w[s, k] for k in range(CONV_K)]
        for b in range(n_streams):
            for r0 in range(0, n_rows, rb):
                acc = [bias] * stride
                for m in range(stride - 1 + CONV_K):
                    x = ext_c[b * SLABS + s, pl.ds(CONV_OFF + r0 + m, SUBLANES, stride=stride), :]
                    for j in range(stride):
                        k = m - j
                        if 0 <= k < CONV_K:
                            acc[j] = acc[j] + taps[k] * x
                for j in range(stride):
                    y_ref[s, pl.ds(b * n_rows + r0 + j, SUBLANES, stride=stride), :] = acc[j]


def _pool(ext_p, p_ref, n_streams, n_rows):
    rb, stride = _block_rows(n_rows)
    for grp, window in enumerate(POOL_WINDOWS):
        for s in range(grp * SLABS_PER_GROUP, (grp + 1) * SLABS_PER_GROUP):
            for b in range(n_streams):
                for r0 in range(0, n_rows, rb):
                    for j in range(stride):
                        base = POOL_PAD + r0 + j
                        tok = ext_p[b * SLABS + s, pl.ds(base, SUBLANES, stride=stride), :]
                        tot = tok
                        for i in range(1, window):
                            tot = tot + ext_p[b * SLABS + s, pl.ds(base - i, SUBLANES, stride=stride), :]
                        p_ref[s, pl.ds(b * n_rows + r0 + j, SUBLANES, stride=stride), :] = (
                            tot * (1.0 / window) - tok)


def _mix_tail(h, w, sc, n_streams, n_rows):
    _causal_dwconv(sc.ext_c, w, sc.y, n_streams, n_rows)
    _pool(sc.ext_p, sc.p, n_streams, n_rows)
    y = jnp.concatenate([sc.y[s] for s in range(SLABS)], axis=-1)
    cn = _rmsnorm(y, w.conv_norm[...])
    sc.cp[:, :CONV_CH] = (cn * jax.nn.sigmoid(cn)).astype(jnp.bfloat16)
    for grp in range(len(POOL_WINDOWS)):
        lanes = slice(grp * POOL_GROUP, (grp + 1) * POOL_GROUP)
        slabs = range(grp * SLABS_PER_GROUP, (grp + 1) * SLABS_PER_GROUP)
        p = jnp.concatenate([sc.p[s] for s in slabs], axis=-1).astype(jnp.bfloat16)
        sc.cp[:, CONV_CH + grp * POOL_GROUP:CONV_CH + (grp + 1) * POOL_GROUP] = (
            _dot(p, w.pool_w[grp].astype(jnp.bfloat16)) * w.pool_scale[:, lanes]).astype(jnp.bfloat16)
    return jnp.concatenate(
        [h[:, c * WEIGHT_WINDOW:(c + 1) * WEIGHT_WINDOW] + _dot(sc.cp[...], w_ref[...])
         for c, w_ref in enumerate(w.w_out)], axis=-1)


N_MIX_SCRATCH = len(_MixScratch._fields)


def _mix_prompt_kernel(h_ref, hc_ref, hp_ref, *refs):
    w = _mix_weight_refs(refs[:N_MIX_WEIGHT_REFS])
    o_ref, ct_ref, pt_ref = refs[N_MIX_WEIGHT_REFS:-N_MIX_SCRATCH]
    sc = _MixScratch(*refs[-N_MIX_SCRATCH:])
    i = pl.program_id(0)
    tm = h_ref.shape[0]

    @pl.when(i == 0)
    def _():
        sc.ext_c[:, 0:CONV_PAD, :] = hc_ref[...]
        sc.ext_p[:, 0:POOL_PAD, :] = hp_ref[...]

    @pl.when(i > 0)
    def _():
        sc.ext_c[:, 0:CONV_PAD, :] = sc.ext_c[:, tm:tm + CONV_PAD, :]
        sc.ext_p[:, 0:POOL_PAD, :] = sc.ext_p[:, tm:tm + POOL_PAD, :]

    h = h_ref[...]
    u_conv, u_pool = _mix_inputs(h, w, sc.hn)
    for s in range(SLABS):
        sc.ext_c[s, CONV_PAD:, :] = _slab(u_conv, s)
        sc.ext_p[s, POOL_PAD:, :] = _slab(u_pool, s)
    o_ref[...] = _mix_tail(h, w, sc, 1, tm)
    ct_ref[...] = sc.ext_c[:, tm:tm + CONV_PAD, :]
    pt_ref[...] = sc.ext_p[:, tm:tm + POOL_PAD, :]


def _mix_small_kernel(h_ref, hm_ref, sc_ref, sp_ref, *refs):
    w = _mix_weight_refs(refs[:N_MIX_WEIGHT_REFS])
    o_ref, nc_ref, np_ref, mc_ref, mp_ref = refs[N_MIX_WEIGHT_REFS:-N_MIX_SCRATCH]
    sc = _MixScratch(*refs[-N_MIX_SCRATCH:])
    n_streams = sc_ref.shape[0]
    n_rows = sc.ext_c.shape[1] - CONV_PAD

    @pl.when(pl.program_id(0) == 0)
    def _():
        m_conv, m_pool = _mix_inputs(hm_ref[...], w, sc.hn.at[pl.ds(0, N_META)])
        for s in range(SLABS):
            mc_ref[s] = _slab(m_conv, s)
            mp_ref[s] = _slab(m_pool, s)

    h = h_ref[...]
    u_conv, u_pool = _mix_inputs(h, w, sc.hn)
    for s in range(SLABS):
        for b in range(n_streams):
            rows = slice(b * n_rows, (b + 1) * n_rows)
            sc.ext_c[b * SLABS + s, CONV_OFF:CONV_PAD, :] = sc_ref[b, :, _lanes(s)]
            sc.ext_p[b * SLABS + s, POOL_OFF:POOL_PAD, :] = sp_ref[b, :, _lanes(s)]
            sc.ext_c[b * SLABS + s, CONV_PAD:, :] = _slab(u_conv, s, rows)
            sc.ext_p[b * SLABS + s, POOL_PAD:, :] = _slab(u_pool, s, rows)

    o_ref[...] = _mix_tail(h, w, sc, n_streams, n_rows)
    for s in range(SLABS):
        for b in range(n_streams):
            nc_ref[b, :, _lanes(s)] = sc.ext_c[b * SLABS + s, n_rows + CONV_OFF:, :]
            np_ref[b, :, _lanes(s)] = sc.ext_p[b * SLABS + s, n_rows + POOL_OFF:, :]


def _const_spec(shape):
    zeros = (0,) * len(shape)
    return pl.BlockSpec(shape, lambda *_: zeros, pipeline_mode=pl.Buffered(1))


def _mix_prompt(h, hist_c, hist_p, weights):
    t, d = h.shape
    tm = MIX_ROWS
    assert t % tm == 0
    w_args, w_specs = _mix_weight_operands(weights)
    return pl.pallas_call(
        _mix_prompt_kernel,
        out_shape=(jax.ShapeDtypeStruct((t, d), jnp.float32),
                   jax.ShapeDtypeStruct((SLABS, CONV_PAD, LANES), jnp.float32),
                   jax.ShapeDtypeStruct((SLABS, POOL_PAD, LANES), jnp.float32)),
        grid=(t // tm,),
        in_specs=[pl.BlockSpec((tm, d), lambda i: (i, 0)),
                  _const_spec((SLABS, CONV_PAD, LANES)),
                  _const_spec((SLABS, POOL_PAD, LANES))] + w_specs,
        out_specs=(pl.BlockSpec((tm, d), lambda i: (i, 0)),
                   pl.BlockSpec((SLABS, CONV_PAD, LANES), lambda i: (0, 0, 0)),
                   pl.BlockSpec((SLABS, POOL_PAD, LANES), lambda i: (0, 0, 0))),
        scratch_shapes=_mix_scratch_shapes(1, tm),
        compiler_params=pltpu.CompilerParams(
            dimension_semantics=("arbitrary",),
            vmem_limit_bytes=VMEM_LIMIT_BYTES),
        name="mix_prompt",
    )(h, hist_c, hist_p, *w_args)


def _mix_small(h, state_c, state_p, weights):
    t, d = h.shape
    n_streams, _, _ = state_c.shape
    n_tok = t - N_META
    n_rows = n_tok // n_streams
    grp = MIX_SMALL_STREAMS
    tm = grp * n_rows
    assert n_rows * n_streams == n_tok and n_rows % SUBLANES == 0
    assert n_streams % grp == 0 and n_tok % N_META == 0 and tm >= N_META
    w_args, w_specs = _mix_weight_operands(weights)
    return pl.pallas_call(
        _mix_small_kernel,
        out_shape=(jax.ShapeDtypeStruct((n_tok, d), jnp.float32),
                   jax.ShapeDtypeStruct((n_streams, CONV_HIST, CONV_CH), jnp.float32),
                   jax.ShapeDtypeStruct((n_streams, POOL_HIST, POOL_CH), jnp.float32),
                   jax.ShapeDtypeStruct((SLABS, N_META, LANES), jnp.float32),
                   jax.ShapeDtypeStruct((SLABS, N_META, LANES), jnp.float32)),
        grid=(n_streams // grp,),
        in_specs=[pl.BlockSpec((tm, d), lambda i: (i, 0)),
                  pl.BlockSpec((N_META, d), lambda i: (n_tok // N_META, 0), pipeline_mode=pl.Buffered(1)),
                  pl.BlockSpec((grp, CONV_HIST, CONV_CH), lambda i: (i, 0, 0)),
                  pl.BlockSpec((grp, POOL_HIST, POOL_CH), lambda i: (i, 0, 0))]
        + w_specs,
        out_specs=(pl.BlockSpec((tm, d), lambda i: (i, 0)),
                   pl.BlockSpec((grp, CONV_HIST, CONV_CH), lambda i: (i, 0, 0)),
                   pl.BlockSpec((grp, POOL_HIST, POOL_CH), lambda i: (i, 0, 0)),
                   pl.BlockSpec((SLABS, N_META, LANES), lambda i: (0, 0, 0)),
                   pl.BlockSpec((SLABS, N_META, LANES), lambda i: (0, 0, 0))),
        scratch_shapes=_mix_scratch_shapes(grp, n_rows),
        compiler_params=pltpu.CompilerParams(
            dimension_semantics=("arbitrary",),
            vmem_limit_bytes=VMEM_LIMIT_BYTES),
        name="mix_small",
    )(h, h, state_c, state_p, *w_args)


def _to_slabs(v):
    return v.reshape(v.shape[0], SLABS, LANES).transpose(1, 0, 2)


def _from_slabs(v):
    return v.transpose(1, 0, 2).reshape(v.shape[1], SLABS * LANES)


def kernel(x_prompt, x_sample, state_conv, state_pool, meta_tokens, ffn1_norm, ffn1_w_gate, ffn1_w_up, ffn1_w_down, mix_norm, w_in, conv_w, conv_b, conv_norm, pool_w, pool_scale, w_out, ffn2_norm, ffn2_w_gate, ffn2_w_up, ffn2_w_down, final_norm):
    depth = ffn1_norm.shape[0]
    b_p, seq, d = x_prompt.shape
    b_s, seq_s, _ = x_sample.shape
    assert depth == 1 and b_p == 1

    xp = x_prompt.reshape(seq, d)
    xs = jnp.concatenate([x_sample.reshape(b_s * seq_s, d), meta_tokens.astype(x_sample.dtype)], axis=0)

    row = lambda v: v.reshape(1, -1)
    fg = row(final_norm)
    hs, wg1, wu1, wd1 = _ffn(xs, row(ffn1_norm[0]), ffn1_w_gate[0], ffn1_w_up[0], ffn1_w_down[0], fg,
                             final_norm=False, name="ffn1_small")
    hp, wg2, wu2, wd2, w_in_bf, w_out_bf = _ffn(
        xp, row(ffn1_norm[0]), wg1, wu1, wd1, fg,
        cast=(ffn2_w_gate[0], ffn2_w_up[0], ffn2_w_down[0], w_in[0], w_out[0]),
        final_norm=False, name="ffn1_prompt")
    ffn2 = (row(ffn2_norm[0]), wg2, wu2, wd2, fg)
    mixw = _MixWeights(
        norm=row(mix_norm[0]),
        w_in=w_in_bf,
        conv_w=jnp.broadcast_to(_to_slabs(conv_w[0])[:, :, None, :], (SLABS, CONV_K, SUBLANES, LANES)),
        conv_b=jnp.broadcast_to(_to_slabs(row(conv_b[0])), (SLABS, SUBLANES, LANES)),
        conv_norm=row(conv_norm[0]),
        pool_w=pool_w[0],
        pool_scale=row(pool_scale[0]),
        w_out=w_out_bf)

    hs, new_conv_s, new_pool_s, meta_c, meta_p = _mix_small(hs, state_conv[0], state_pool[0], mixw)
    hist_c = jnp.concatenate([jnp.zeros((SLABS, CONV_PAD - N_META, LANES), jnp.float32), meta_c], axis=1)
    hp, conv_tail, pool_tail = _mix_prompt(hp, hist_c, meta_p, mixw)

    ys = _ffn(hs, *ffn2, final_norm=True, name="ffn2_small")
    yp = _ffn(hp, *ffn2, final_norm=True, name="ffn2_prompt")

    y_prompt = yp.reshape(b_p, seq, d)
    y_sample = ys.reshape(b_s, seq_s, d)
    new_conv_prompt = _from_slabs(conv_tail[:, CONV_OFF:, :])[None, None]
    new_pool_prompt = _from_slabs(pool_tail[:, POOL_OFF:, :])[None, None]
    return (y_prompt, y_sample, new_conv_prompt, new_pool_prompt, new_conv_s[None], new_pool_s[None])
```

```python
import functools
from typing import Any, NamedTuple

import jax
import jax.numpy as jnp
from jax import lax
from jax.experimental import pallas as pl
from jax.experimental.pallas import tpu as pltpu

D_MODEL = 2048
N_META = 16
CONV_CH = 1024
POOL_CH = 1024
MIX_WIDTH = CONV_CH + POOL_CH
CONV_K = 31
CONV_HIST = CONV_K - 1
POOL_WINDOWS = (2, 4, 8, 16)
POOL_GROUP = POOL_CH // len(POOL_WINDOWS)
POOL_HIST = max(POOL_WINDOWS) - 1
EPS = 1e-6

SUBLANES = 8
BF16_SUBLANES = 16
LANES = 128
VMEM_LIMIT_BYTES = 60 * 1024 * 1024
WEIGHT_WINDOW = 512

SLABS = CONV_CH // LANES
assert POOL_CH == CONV_CH and POOL_GROUP % LANES == 0
SLABS_PER_GROUP = POOL_GROUP // LANES

CONV_PAD = 32
POOL_PAD = 16
CONV_OFF = CONV_PAD - CONV_HIST
POOL_OFF = POOL_PAD - POOL_HIST

FFN_ROWS = 1024
FFN_COLS = 512
FFN_COLS_F32 = 512
MIX_ROWS = 512
Z_COLS = 256
MIX_SMALL_STREAMS = 8
MAX_BLOCK_ROWS = 4 * SUBLANES


def _rmsnorm(x, g):
    ms = jnp.mean(x * x, axis=-1, keepdims=True)
    return x * lax.rsqrt(ms + EPS) * g


def _dot(a, b):
    return jnp.dot(a, b, preferred_element_type=jnp.float32)


def _ffn_kernel(x_hbm, g_ref, wg_ref, wu_ref, wd_ref, fg_ref, *refs, final_norm, n_cast, emit_weights):
    n_own = 3 if emit_weights else 0
    cast_in, o_ref = refs[:n_cast], refs[n_cast]
    xn_ref, x_buf, x_sem = refs[-3:]
    own_out, cast_out = refs[n_cast + 1:n_cast + 1 + n_own], refs[n_cast + 1 + n_own:-3]
    i, j = pl.program_id(0), pl.program_id(1)
    tm = x_buf.shape[0]

    def x_copy(tile):
        return pltpu.make_async_copy(x_hbm.at[pl.ds(tile * tm, tm), :], x_buf, x_sem)

    @pl.when(j == 0)
    def _():
        @pl.when(i == 0)
        def _():
            x_copy(0).start()

        x_copy(i).wait()
        x = x_buf[...]
        xn_ref[...] = _rmsnorm(x, g_ref[...]).astype(jnp.bfloat16)
        o_ref[...] = x

        @pl.when(i + 1 < pl.num_programs(0))
        def _():
            x_copy(i + 1).start()

    wg, wu, wd = wg_ref[...], wu_ref[...], wd_ref[...]
    if emit_weights:
        wg, wu, wd = (w.astype(jnp.bfloat16) for w in (wg, wu, wd))
        for dst, w in zip(own_out, (wg, wu, wd)):
            dst[...] = w

    xn = xn_ref[...]
    gate = _dot(xn, wg)
    up = _dot(xn, wu)
    hid = (0.5 * (gate * jax.nn.sigmoid(gate)) * up).astype(jnp.bfloat16)
    o_ref[...] += _dot(hid, wd)

    for src, dst in zip(cast_in, cast_out):
        dst[...] = src[...].astype(jnp.bfloat16)

    if final_norm:
        @pl.when(j == pl.num_programs(1) - 1)
        def _():
            o_ref[...] = _rmsnorm(o_ref[...], fg_ref[...])


def _cast_spec(shape, ni, nj):
    r, c = shape
    if r % ni == 0 and c % nj == 0 and (c // nj) % LANES == 0:
        return pl.BlockSpec((r // ni, c // nj), lambda i, j: (i, j))
    if r % nj == 0 and c % ni == 0 and (c // ni) % LANES == 0:
        return pl.BlockSpec((r // nj, c // ni), lambda i, j: (j, i))
    nb = max(n for n in range(1, nj + 1) if c % n == 0 and (c // n) % LANES == 0)
    assert r % ni == 0 and (r // ni) % BF16_SUBLANES == 0
    return pl.BlockSpec((r // ni, c // nb), lambda i, j: (i, jnp.minimum(j, nb - 1)))


def _window_cast_specs(shape, ni, nj):
    k, n = shape
    nb = n // WEIGHT_WINDOW
    assert n % WEIGHT_WINDOW == 0 and nb <= nj and k % ni == 0 and (k // ni) % BF16_SUBLANES == 0
    return (pl.BlockSpec((k // ni, WEIGHT_WINDOW), lambda i, j: (i, jnp.minimum(j, nb - 1))),
            pl.BlockSpec((None, k // ni, WEIGHT_WINDOW), lambda i, j: (jnp.minimum(j, nb - 1), i, 0)),
            (nb, k, WEIGHT_WINDOW))


def _ffn(x, g, wg, wu, wd, fg, cast=(), cast_windows=(), *, final_norm, name):
    t, d = x.shape
    f = wg.shape[1]
    emit_weights = wg.dtype == jnp.float32
    tm = FFN_ROWS if t % FFN_ROWS == 0 else t
    tf = FFN_COLS_F32 if emit_weights else FFN_COLS
    assert t % tm == 0 and f % tf == 0
    grid = (t // tm, f // tf)
    assert not emit_weights or grid[0] == 1
    w_specs = [pl.BlockSpec((d, tf), lambda i, j: (0, j)),
               pl.BlockSpec((d, tf), lambda i, j: (0, j)),
               pl.BlockSpec((tf, d), lambda i, j: (j, 0))]
    own = [wg, wu, wd] if emit_weights else []
    cast_specs = [_cast_spec(w.shape, *grid) for w in cast]
    win = [_window_cast_specs(w.shape, *grid) for w in cast_windows]
    cast_in_specs = cast_specs + [a for a, _, _ in win]
    cast_out_specs = cast_specs + [b for _, b, _ in win]
    out = pl.pallas_call(
        functools.partial(_ffn_kernel, final_norm=final_norm, n_cast=len(cast) + len(win),
                          emit_weights=emit_weights),
        out_shape=[jax.ShapeDtypeStruct((t, d), jnp.float32)]
        + [jax.ShapeDtypeStruct(w.shape, jnp.bfloat16) for w in own + list(cast)]
        + [jax.ShapeDtypeStruct(shape, jnp.bfloat16) for _, _, shape in win],
        grid=grid,
        in_specs=[pl.BlockSpec(memory_space=pl.ANY),
                  pl.BlockSpec((1, d), lambda i, j: (0, 0))]
        + w_specs + [pl.BlockSpec((1, d), lambda i, j: (0, 0))] + cast_in_specs,
        out_specs=[pl.BlockSpec((tm, d), lambda i, j: (i, 0))] + w_specs[:len(own)] + cast_out_specs,
        scratch_shapes=[pltpu.VMEM((tm, d), jnp.bfloat16),
                        pltpu.VMEM((tm, d), jnp.float32),
                        pltpu.SemaphoreType.DMA(())],
        compiler_params=pltpu.CompilerParams(
            dimension_semantics=("arbitrary", "arbitrary"),
            vmem_limit_bytes=VMEM_LIMIT_BYTES),
        name=name,
    )(x, g, wg, wu, wd, fg, *cast, *cast_windows)
    return out[0] if len(out) == 1 else out


class _MixWeights(NamedTuple):
    norm: Any
    w_in: Any
    conv_w: Any
    conv_b: Any
    conv_norm: Any
    pool_w: Any
    pool_scale: Any
    w_out: Any


def _mix_weight_operands(w, whole_w_in):
    def windows(v):
        n, k, c = v.shape
        return [(v, pl.BlockSpec((None, k, c), functools.partial(lambda m, *_: (m, 0, 0), m),
                                 pipeline_mode=pl.Buffered(1))) for m in range(n)]

    pairs = []
    for name, v in w._asdict().items():
        split = name == "w_out" or (name == "w_in" and not whole_w_in)
        pairs += windows(v) if split else [(v, _const_spec(v.shape))]
    return [v for v, _ in pairs], [s for _, s in pairs]


N_IN_WINDOWS = (2 * CONV_CH + POOL_CH) // WEIGHT_WINDOW
N_OUT_WINDOWS = D_MODEL // WEIGHT_WINDOW


def _n_mix_weight_refs(whole_w_in):
    return len(_MixWeights._fields) - 2 + (1 if whole_w_in else N_IN_WINDOWS) + N_OUT_WINDOWS


def _mix_weight_refs(refs, whole_w_in):
    refs = list(refs)
    assert len(refs) == _n_mix_weight_refs(whole_w_in)
    take = lambda n: tuple(refs.pop(0) for _ in range(n))
    return _MixWeights(**{name: take(N_IN_WINDOWS) if name == "w_in" and not whole_w_in else
                          take(N_OUT_WINDOWS) if name == "w_out" else refs.pop(0)
                          for name in _MixWeights._fields})


class _MixScratch(NamedTuple):
    ext_c: Any
    ext_p: Any
    y: Any
    p: Any
    hn: Any
    cp: Any
    z: Any


def _mix_scratch_shapes(n_streams, n_rows):
    rows = n_streams * n_rows
    return [pltpu.VMEM((n_streams * SLABS, CONV_PAD + n_rows, LANES), jnp.float32),
            pltpu.VMEM((n_streams * SLABS, POOL_PAD + n_rows, LANES), jnp.float32),
            pltpu.VMEM((SLABS, rows, LANES), jnp.float32),
            pltpu.VMEM((SLABS, rows if n_streams > 1 else SUBLANES, LANES), jnp.float32),
            pltpu.VMEM((rows, D_MODEL), jnp.bfloat16),
            pltpu.VMEM((rows, MIX_WIDTH), jnp.bfloat16),
            pltpu.VMEM((2 * CONV_CH // Z_COLS, rows if n_streams == 1 else SUBLANES, Z_COLS),
                       jnp.float32)]


def _block_rows(n_rows):
    rb = min(MAX_BLOCK_ROWS, n_rows)
    assert n_rows % rb == 0 and rb % SUBLANES == 0
    return rb, rb // SUBLANES


def _lanes(slab):
    return slice(slab * LANES, (slab + 1) * LANES)


def _mix_inputs(h, w, hn_ref):
    hn_ref[...] = _rmsnorm(h, w.norm[...]).astype(jnp.bfloat16)
    nc = CONV_CH // WEIGHT_WINDOW
    u_conv = []
    for c in range(nc):
        a = _dot(hn_ref[...], w.w_in[c][...])
        gate = _dot(hn_ref[...], w.w_in[nc + c][...])
        u_conv.append(a * jax.nn.sigmoid(gate))
    u_pool = [_dot(hn_ref[...], w_ref[...]) for w_ref in w.w_in[2 * nc:]]
    return u_conv, u_pool


def _slab(chunks, s, rows=slice(None)):
    per = WEIGHT_WINDOW // LANES
    return chunks[s // per][rows, _lanes(s % per)]


def _causal_dwconv(ext_c, w, y_ref, n_streams, n_rows, slabs=range(SLABS)):
    rb, stride = _block_rows(n_rows)
    for s in slabs:
        bias = w.conv_b[s]
        for b in range(n_streams):
            for r0 in range(0, n_rows, rb):
                acc = [bias] * stride
                for m in range(stride - 1 + CONV_K):
                    x = ext_c[b * SLABS + s, pl.ds(CONV_OFF + r0 + m, SUBLANES, stride=stride), :]
                    for j in range(stride):
                        k = m - j
                        if 0 <= k < CONV_K:
                            acc[j] = acc[j] + w.conv_w[s, k] * x
                for j in range(stride):
                    y_ref[s, pl.ds(b * n_rows + r0 + j, SUBLANES, stride=stride), :] = acc[j]


def _pool(ext_p, p_ref, n_streams, n_rows):
    rb, stride = _block_rows(n_rows)
    for grp, window in enumerate(POOL_WINDOWS):
        for s in range(grp * SLABS_PER_GROUP, (grp + 1) * SLABS_PER_GROUP):
            for b in range(n_streams):
                for r0 in range(0, n_rows, rb):
                    for j in range(stride):
                        base = POOL_PAD + r0 + j
                        tok = ext_p[b * SLABS + s, pl.ds(base, SUBLANES, stride=stride), :]
                        tot = tok
                        for i in range(1, window):
                            tot = tot + ext_p[b * SLABS + s, pl.ds(base - i, SUBLANES, stride=stride), :]
                        p_ref[s, pl.ds(b * n_rows + r0 + j, SUBLANES, stride=stride), :] = (
                            tot * (1.0 / window) - tok)


def _mix_tail(h, w, sc, n_streams, n_rows, conv_done=False):
    if not conv_done:
        _causal_dwconv(sc.ext_c, w, sc.y, n_streams, n_rows)
    y = jnp.concatenate([sc.y[s] for s in range(SLABS)], axis=-1)
    cn = _rmsnorm(y, w.conv_norm[...])
    sc.cp[:, :CONV_CH] = (cn * jax.nn.sigmoid(cn)).astype(jnp.bfloat16)
    _pool(sc.ext_p, sc.p, n_streams, n_rows)
    for grp in range(len(POOL_WINDOWS)):
        lanes = slice(grp * POOL_GROUP, (grp + 1) * POOL_GROUP)
        slabs = range(grp * SLABS_PER_GROUP, (grp + 1) * SLABS_PER_GROUP)
        p = jnp.concatenate([sc.p[s] for s in slabs], axis=-1).astype(jnp.bfloat16)
        sc.cp[:, CONV_CH + grp * POOL_GROUP:CONV_CH + (grp + 1) * POOL_GROUP] = (
            _dot(p, w.pool_w[grp]) * w.pool_scale[:, lanes]).astype(jnp.bfloat16)
    return jnp.concatenate(
        [h[:, c * WEIGHT_WINDOW:(c + 1) * WEIGHT_WINDOW] + _dot(sc.cp[...], w_ref[...])
         for c, w_ref in enumerate(w.w_out)], axis=-1)


N_MIX_SCRATCH = len(_MixScratch._fields)


def _mix_prompt_kernel(h_ref, hc_ref, hp_ref, *refs):
    n_w = _n_mix_weight_refs(True)
    w = _mix_weight_refs(refs[:n_w], True)
    o_ref, ct_ref, pt_ref = refs[n_w:-N_MIX_SCRATCH]
    sc = _MixScratch(*refs[-N_MIX_SCRATCH:])
    sc = sc._replace(p=sc.y)
    i = pl.program_id(0)
    tm = h_ref.shape[0]

    @pl.when(i == 0)
    def _():
        sc.ext_c[:, 0:CONV_PAD, :] = hc_ref[...]
        sc.ext_p[:, 0:POOL_PAD, :] = hp_ref[...]

    @pl.when(i > 0)
    def _():
        sc.ext_c[:, 0:CONV_PAD, :] = sc.ext_c[:, tm:tm + CONV_PAD, :]
        sc.ext_p[:, 0:POOL_PAD, :] = sc.ext_p[:, tm:tm + POOL_PAD, :]

    win = WEIGHT_WINDOW
    n_cw = CONV_CH // win
    zpw = win // Z_COLS
    assert n_cw == 2 and N_IN_WINDOWS == 3 * n_cw

    def project(c):
        res = _dot(sc.hn[...], w.w_in[c])
        for k in range(zpw):
            sc.z[c * zpw + k] = res[:, k * Z_COLS:(k + 1) * Z_COLS]

    sc.hn[...] = _rmsnorm(h_ref[...], w.norm[...]).astype(jnp.bfloat16)
    project(0)
    project(n_cw)
    n_iter = N_IN_WINDOWS - 2
    group = SLABS // n_iter

    def glu_to_slabs(c):
        per = win // LANES
        for k in range(zpw):
            u = sc.z[c * zpw + k] * jax.nn.sigmoid(sc.z[(n_cw + c) * zpw + k])
            for q in range(Z_COLS // LANES):
                sc.ext_c[c * per + k * (Z_COLS // LANES) + q, CONV_PAD:, :] = u[:, _lanes(q)]

    def paired(r, carry):
        project(jnp.where(r < n_cw - 1, r + 1, jnp.where(r < 2 * (n_cw - 1), r + n_cw, r + 2)))
        _causal_dwconv(sc.ext_c, w, sc.y, 1, tm, [r * group + k for k in range(group)])
        return carry

    def paired_pool(r, carry):
        c = jnp.where(r < 2 * n_cw - 1, 2 * n_cw, 2 * n_cw + 1)
        res = _dot(sc.hn[...], w.w_in[c])
        for q in range(win // LANES):
            sc.ext_p[(c - 2 * n_cw) * (win // LANES) + q, POOL_PAD:, :] = res[:, _lanes(q)]
        _causal_dwconv(sc.ext_c, w, sc.y, 1, tm, [r * group + k for k in range(group)])
        return carry

    per_glu = n_iter // n_cw
    glu_to_slabs(0)
    lax.fori_loop(0, per_glu, paired, 0)
    glu_to_slabs(1)
    lax.fori_loop(per_glu, 2 * per_glu, paired_pool, 0)
    o_ref[...] = _mix_tail(h_ref[...], w, sc, 1, tm, conv_done=True)
    ct_ref[...] = sc.ext_c[:, tm:tm + CONV_PAD, :]
    pt_ref[...] = sc.ext_p[:, tm:tm + POOL_PAD, :]


def _mix_small_kernel(h_ref, hm_ref, sc_ref, sp_ref, *refs):
    n_w = _n_mix_weight_refs(False)
    w = _mix_weight_refs(refs[:n_w], False)
    o_ref, nc_ref, np_ref, mc_ref, mp_ref = refs[n_w:-N_MIX_SCRATCH]
    sc = _MixScratch(*refs[-N_MIX_SCRATCH:])
    n_streams = sc_ref.shape[0]
    n_rows = sc.ext_c.shape[1] - CONV_PAD

    @pl.when(pl.program_id(0) == 0)
    def _():
        m_conv, m_pool = _mix_inputs(hm_ref[...], w, sc.hn.at[pl.ds(0, N_META)])
        for s in range(SLABS):
            mc_ref[s] = _slab(m_conv, s)
            mp_ref[s] = _slab(m_pool, s)

    h = h_ref[...]
    u_conv, u_pool = _mix_inputs(h, w, sc.hn)
    for s in range(SLABS):
        for b in range(n_streams):
            rows = slice(b * n_rows, (b + 1) * n_rows)
            sc.ext_c[b * SLABS + s, CONV_OFF:CONV_PAD, :] = sc_ref[b, :, _lanes(s)]
            sc.ext_p[b * SLABS + s, POOL_OFF:POOL_PAD, :] = sp_ref[b, :, _lanes(s)]
            sc.ext_c[b * SLABS + s, CONV_PAD:, :] = _slab(u_conv, s, rows)
            sc.ext_p[b * SLABS + s, POOL_PAD:, :] = _slab(u_pool, s, rows)

    o_ref[...] = _mix_tail(h, w, sc, n_streams, n_rows)
    for s in range(SLABS):
        for b in range(n_streams):
            nc_ref[b, :, _lanes(s)] = sc.ext_c[b * SLABS + s, n_rows + CONV_OFF:, :]
            np_ref[b, :, _lanes(s)] = sc.ext_p[b * SLABS + s, n_rows + POOL_OFF:, :]


def _const_spec(shape):
    zeros = (0,) * len(shape)
    return pl.BlockSpec(shape, lambda *_: zeros, pipeline_mode=pl.Buffered(1))


def _mix_prompt(h, hist_c, hist_p, weights):
    t, d = h.shape
    tm = MIX_ROWS
    assert t % tm == 0
    w_args, w_specs = _mix_weight_operands(weights, True)
    return pl.pallas_call(
        _mix_prompt_kernel,
        out_shape=(jax.ShapeDtypeStruct((t, d), jnp.float32),
                   jax.ShapeDtypeStruct((SLABS, CONV_PAD, LANES), jnp.float32),
                   jax.ShapeDtypeStruct((SLABS, POOL_PAD, LANES), jnp.float32)),
        grid=(t // tm,),
        in_specs=[pl.BlockSpec((tm, d), lambda i: (i, 0)),
                  _const_spec((SLABS, CONV_PAD, LANES)),
                  _const_spec((SLABS, POOL_PAD, LANES))] + w_specs,
        out_specs=(pl.BlockSpec((tm, d), lambda i: (i, 0)),
                   pl.BlockSpec((SLABS, CONV_PAD, LANES), lambda i: (0, 0, 0)),
                   pl.BlockSpec((SLABS, POOL_PAD, LANES), lambda i: (0, 0, 0))),
        scratch_shapes=_mix_scratch_shapes(1, tm),
        compiler_params=pltpu.CompilerParams(
            dimension_semantics=("arbitrary",),
            vmem_limit_bytes=VMEM_LIMIT_BYTES),
        name="mix_prompt",
    )(h, hist_c, hist_p, *w_args)


def _mix_small(h, state_c, state_p, weights):
    t, d = h.shape
    n_streams, _, _ = state_c.shape
    n_tok = t - N_META
    n_rows = n_tok // n_streams
    grp = MIX_SMALL_STREAMS
    tm = grp * n_rows
    assert n_rows * n_streams == n_tok and n_rows % SUBLANES == 0
    assert n_streams % grp == 0 and n_tok % N_META == 0 and tm >= N_META
    w_args, w_specs = _mix_weight_operands(weights, False)
    return pl.pallas_call(
        _mix_small_kernel,
        out_shape=(jax.ShapeDtypeStruct((n_tok, d), jnp.float32),
                   jax.ShapeDtypeStruct((n_streams, CONV_HIST, CONV_CH), jnp.float32),
                   jax.ShapeDtypeStruct((n_streams, POOL_HIST, POOL_CH), jnp.float32),
                   jax.ShapeDtypeStruct((SLABS, N_META, LANES), jnp.float32),
                   jax.ShapeDtypeStruct((SLABS, N_META, LANES), jnp.float32)),
        grid=(n_streams // grp,),
        in_specs=[pl.BlockSpec((tm, d), lambda i: (i, 0)),
                  pl.BlockSpec((N_META, d), lambda i: (n_tok // N_META, 0), pipeline_mode=pl.Buffered(1)),
                  pl.BlockSpec((grp, CONV_HIST, CONV_CH), lambda i: (i, 0, 0)),
                  pl.BlockSpec((grp, POOL_HIST, POOL_CH), lambda i: (i, 0, 0))]
        + w_specs,
        out_specs=(pl.BlockSpec((tm, d), lambda i: (i, 0)),
                   pl.BlockSpec((grp, CONV_HIST, CONV_CH), lambda i: (i, 0, 0)),
                   pl.BlockSpec((grp, POOL_HIST, POOL_CH), lambda i: (i, 0, 0)),
                   pl.BlockSpec((SLABS, N_META, LANES), lambda i: (0, 0, 0)),
                   pl.BlockSpec((SLABS, N_META, LANES), lambda i: (0, 0, 0))),
        scratch_shapes=_mix_scratch_shapes(grp, n_rows),
        compiler_params=pltpu.CompilerParams(
            dimension_semantics=("arbitrary",),
            vmem_limit_bytes=VMEM_LIMIT_BYTES),
        name="mix_small",
    )(h, h, state_c, state_p, *w_args)


def _to_slabs(v):
    return v.reshape(v.shape[0], SLABS, LANES).transpose(1, 0, 2)


def _from_slabs(v):
    return v.transpose(1, 0, 2).reshape(v.shape[1], SLABS * LANES)


def kernel(x_prompt, x_sample, state_conv, state_pool, meta_tokens, ffn1_norm, ffn1_w_gate, ffn1_w_up, ffn1_w_down, mix_norm, w_in, conv_w, conv_b, conv_norm, pool_w, pool_scale, w_out, ffn2_norm, ffn2_w_gate, ffn2_w_up, ffn2_w_down, final_norm):
    depth = ffn1_norm.shape[0]
    b_p, seq, d = x_prompt.shape
    b_s, seq_s, _ = x_sample.shape
    assert depth == 1 and b_p == 1
    bf = jnp.bfloat16

    xp = x_prompt.reshape(seq, d)
    xs = jnp.concatenate([x_sample.reshape(b_s * seq_s, d), meta_tokens.astype(x_sample.dtype)], axis=0)

    row = lambda v: v.reshape(1, -1)
    fg = row(final_norm)
    hs, wg1, wu1, wd1 = _ffn(xs, row(ffn1_norm[0]), ffn1_w_gate[0], ffn1_w_up[0], ffn1_w_down[0], fg,
                             final_norm=False, name="ffn1_small")
    hp, wg2, wu2, wd2, w_in_bf, w_out_bf = _ffn(
        xp, row(ffn1_norm[0]), wg1, wu1, wd1, fg,
        cast=(ffn2_w_gate[0], ffn2_w_up[0], ffn2_w_down[0]), cast_windows=(w_in[0], w_out[0]),
        final_norm=False, name="ffn1_prompt")
    ffn2 = (row(ffn2_norm[0]), wg2, wu2, wd2, fg)
    mixw = _MixWeights(
        norm=row(mix_norm[0]),
        w_in=w_in_bf,
        conv_w=jnp.broadcast_to(_to_slabs(conv_w[0])[:, :, None, :], (SLABS, CONV_K, SUBLANES, LANES)),
        conv_b=jnp.broadcast_to(_to_slabs(row(conv_b[0])), (SLABS, SUBLANES, LANES)),
        conv_norm=row(conv_norm[0]),
        pool_w=pool_w[0].astype(bf),
        pool_scale=row(pool_scale[0]),
        w_out=w_out_bf)

    hs, new_conv_s, new_pool_s, meta_c, meta_p = _mix_small(hs, state_conv[0], state_pool[0], mixw)
    hist_c = jnp.concatenate([jnp.zeros((SLABS, CONV_PAD - N_META, LANES), jnp.float32), meta_c], axis=1)
    hp, conv_tail, pool_tail = _mix_prompt(hp, hist_c, meta_p, mixw)

    ys = _ffn(hs, *ffn2, final_norm=True, name="ffn2_small")
    yp = _ffn(hp, *ffn2, final_norm=True, name="ffn2_prompt")

    y_prompt = yp.reshape(b_p, seq, d)
    y_sample = ys.reshape(b_s, seq_s, d)
    new_conv_prompt = _from_slabs(conv_tail[:, CONV_OFF:, :])[None, None]
    new_pool_prompt = _from_slabs(pool_tail[:, POOL_OFF:, :])[None, None]
    return (y_prompt, y_sample, new_conv_prompt, new_pool_prompt, new_conv_s[None], new_pool_s[None])
```

```python
import functools
from typing import Any, NamedTuple

import jax
import jax.numpy as jnp
from jax import lax
from jax.experimental import pallas as pl
from jax.experimental.pallas import tpu as pltpu

D_MODEL = 2048
N_META = 16
CONV_CH = 1024
POOL_CH = 1024
MIX_WIDTH = CONV_CH + POOL_CH
CONV_K = 31
CONV_HIST = CONV_K - 1
POOL_WINDOWS = (2, 4, 8, 16)
POOL_GROUP = POOL_CH // len(POOL_WINDOWS)
POOL_HIST = max(POOL_WINDOWS) - 1
EPS = 1e-6

SUBLANES = 8
BF16_SUBLANES = 16
LANES = 128
VMEM_LIMIT_BYTES = 60 * 1024 * 1024
WEIGHT_WINDOW = 512

SLABS = CONV_CH // LANES
assert POOL_CH == CONV_CH and POOL_GROUP % LANES == 0
SLABS_PER_GROUP = POOL_GROUP // LANES

CONV_PAD = 32
POOL_PAD = 16
CONV_OFF = CONV_PAD - CONV_HIST
POOL_OFF = POOL_PAD - POOL_HIST

FFN_ROWS = 1024
FFN_COLS = 512
FFN_COLS_F32 = 512
MIX_ROWS = 512
MIX_SMALL_STREAMS = 8
MAX_BLOCK_ROWS = 4 * SUBLANES


def _rmsnorm(x, g):
    ms = jnp.mean(x * x, axis=-1, keepdims=True)
    return x * lax.rsqrt(ms + EPS) * g


def _dot(a, b):
    return jnp.dot(a, b, preferred_element_type=jnp.float32)


def _ffn_kernel(x_hbm, g_ref, *refs, final_norm, n_cast, n_pairs, emit_weights):
    n_w = 3 if emit_weights else 2
    w_refs, fg_ref, refs = refs[:n_w], refs[n_w], refs[n_w + 1:]
    n_in = n_cast + 2 * n_pairs
    cast_in, pair_in, o_ref = refs[:n_cast], refs[n_cast:n_in], refs[n_in]
    xn_ref, x_buf, x_sem = refs[-3:]
    n_own = 2 if emit_weights else 0
    outs = refs[n_in + 1:-3]
    own_out, cast_out, pair_out = outs[:n_own], outs[n_own:n_own + n_cast], outs[n_own + n_cast:]
    i, j = pl.program_id(0), pl.program_id(1)
    tm = x_buf.shape[0]

    def x_copy(tile):
        return pltpu.make_async_copy(x_hbm.at[pl.ds(tile * tm, tm), :], x_buf, x_sem)

    @pl.when(j == 0)
    def _():
        @pl.when(i == 0)
        def _():
            x_copy(0).start()

        x_copy(i).wait()
        x = x_buf[...]
        xn_ref[...] = _rmsnorm(x, g_ref[...]).astype(jnp.bfloat16)
        o_ref[...] = x

        @pl.when(i + 1 < pl.num_programs(0))
        def _():
            x_copy(i + 1).start()

    if emit_weights:
        wg, wu, wd = (w[...].astype(jnp.bfloat16) for w in w_refs)
        own_out[0][0], own_out[0][1] = wg, wu
        own_out[1][...] = wd
    else:
        wg, wu, wd = w_refs[0][0], w_refs[0][1], w_refs[1][...]

    xn = xn_ref[...]
    gate = _dot(xn, wg)
    up = _dot(xn, wu)
    hid = (0.5 * (gate * jax.nn.sigmoid(gate)) * up).astype(jnp.bfloat16)
    o_ref[...] += _dot(hid, wd)

    for src, dst in zip(cast_in, cast_out):
        dst[...] = src[...].astype(jnp.bfloat16)
    for k, dst in enumerate(pair_out):
        dst[0] = pair_in[2 * k][...].astype(jnp.bfloat16)
        dst[1] = pair_in[2 * k + 1][...].astype(jnp.bfloat16)

    if final_norm:
        @pl.when(j == pl.num_programs(1) - 1)
        def _():
            o_ref[...] = _rmsnorm(o_ref[...], fg_ref[...])


def _cast_block(shape, ni, nj):
    r, c = shape
    if r % ni == 0 and c % nj == 0 and (c // nj) % LANES == 0:
        return (r // ni, c // nj), lambda i, j: (i, j)
    if r % nj == 0 and c % ni == 0 and (c // ni) % LANES == 0:
        return (r // nj, c // ni), lambda i, j: (j, i)
    nb = max(n for n in range(1, nj + 1) if c % n == 0 and (c // n) % LANES == 0)
    assert r % ni == 0 and (r // ni) % BF16_SUBLANES == 0
    return (r // ni, c // nb), lambda i, j: (i, jnp.minimum(j, nb - 1))


def _cast_spec(shape, ni, nj):
    return pl.BlockSpec(*_cast_block(shape, ni, nj))


def _pair_cast_spec(shape, ni, nj):
    block, index = _cast_block(shape, ni, nj)
    return pl.BlockSpec((2,) + block, lambda i, j: (0,) + tuple(index(i, j)))


def _ffn(x, g, weights, fg, cast=(), cast_pairs=(), *, final_norm, name):
    t, d = x.shape
    emit_weights = len(weights) == 3
    f = weights[-1].shape[0]
    tm = FFN_ROWS if t % FFN_ROWS == 0 else t
    tf = FFN_COLS_F32 if emit_weights else FFN_COLS
    assert t % tm == 0 and f % tf == 0
    grid = (t // tm, f // tf)
    assert not emit_weights or grid[0] == 1
    gu_spec = pl.BlockSpec((2, d, tf), lambda i, j: (0, 0, j))
    d_spec = pl.BlockSpec((tf, d), lambda i, j: (j, 0))
    if emit_weights:
        w_specs = [pl.BlockSpec((d, tf), lambda i, j: (0, j))] * 2 + [d_spec]
        own_specs = [gu_spec, d_spec]
        own_shapes = [jax.ShapeDtypeStruct((2, d, f), jnp.bfloat16), jax.ShapeDtypeStruct((f, d), jnp.bfloat16)]
    else:
        w_specs, own_specs, own_shapes = [gu_spec, d_spec], [], []
    cast_specs = [_cast_spec(w.shape, *grid) for w in cast]
    pair_in_specs = [_cast_spec(w.shape, *grid) for pair in cast_pairs for w in pair]
    pair_out_specs = [_pair_cast_spec(a.shape, *grid) for a, _ in cast_pairs]
    out = pl.pallas_call(
        functools.partial(_ffn_kernel, final_norm=final_norm, n_cast=len(cast), n_pairs=len(cast_pairs),
                          emit_weights=emit_weights),
        out_shape=[jax.ShapeDtypeStruct((t, d), jnp.float32)] + own_shapes
        + [jax.ShapeDtypeStruct(w.shape, jnp.bfloat16) for w in cast]
        + [jax.ShapeDtypeStruct((2,) + a.shape, jnp.bfloat16) for a, _ in cast_pairs],
        grid=grid,
        in_specs=[pl.BlockSpec(memory_space=pl.ANY),
                  pl.BlockSpec((1, d), lambda i, j: (0, 0))]
        + w_specs + [pl.BlockSpec((1, d), lambda i, j: (0, 0))] + cast_specs + pair_in_specs,
        out_specs=[pl.BlockSpec((tm, d), lambda i, j: (i, 0))] + own_specs + cast_specs + pair_out_specs,
        scratch_shapes=[pltpu.VMEM((tm, d), jnp.bfloat16),
                        pltpu.VMEM((tm, d), jnp.float32),
                        pltpu.SemaphoreType.DMA(())],
        compiler_params=pltpu.CompilerParams(
            dimension_semantics=("arbitrary", "arbitrary"),
            vmem_limit_bytes=VMEM_LIMIT_BYTES),
        name=name,
    )(x, g, *weights, fg, *cast, *[w for pair in cast_pairs for w in pair])
    return out[0] if len(out) == 1 else out


class _MixWeights(NamedTuple):
    norm: Any
    w_in: Any
    conv_w: Any
    conv_b: Any
    conv_norm: Any
    pool_w: Any
    pool_scale: Any
    w_out: Any


def _mix_weight_operands(w):
    def windows(v):
        k, n = v.shape
        assert n % WEIGHT_WINDOW == 0
        return [(v, pl.BlockSpec((k, WEIGHT_WINDOW), functools.partial(lambda c, *_: (0, c), c),
                                 pipeline_mode=pl.Buffered(1))) for c in range(n // WEIGHT_WINDOW)]

    pairs = []
    for name, v in w._asdict().items():
        pairs += windows(v) if name in ("w_in", "w_out") else [(v, _const_spec(v.shape))]
    return [v for v, _ in pairs], [s for _, s in pairs]


N_IN_WINDOWS = (2 * CONV_CH + POOL_CH) // WEIGHT_WINDOW
N_OUT_WINDOWS = D_MODEL // WEIGHT_WINDOW
N_MIX_WEIGHT_REFS = len(_MixWeights._fields) - 2 + N_IN_WINDOWS + N_OUT_WINDOWS


def _mix_weight_refs(refs):
    refs = list(refs)
    assert len(refs) == N_MIX_WEIGHT_REFS
    take = lambda n: tuple(refs.pop(0) for _ in range(n))
    return _MixWeights(**{name: take(N_IN_WINDOWS) if name == "w_in" else
                          take(N_OUT_WINDOWS) if name == "w_out" else refs.pop(0)
                          for name in _MixWeights._fields})


class _MixScratch(NamedTuple):
    ext_c: Any
    ext_p: Any
    y: Any
    p: Any
    hn: Any
    cp: Any


def _mix_scratch_shapes(n_streams, n_rows):
    rows = n_streams * n_rows
    return [pltpu.VMEM((n_streams * SLABS, CONV_PAD + n_rows, LANES), jnp.float32),
            pltpu.VMEM((n_streams * SLABS, POOL_PAD + n_rows, LANES), jnp.float32),
            pltpu.VMEM((SLABS, rows, LANES), jnp.float32),
            pltpu.VMEM((SLABS, rows, LANES), jnp.float32),
            pltpu.VMEM((rows, D_MODEL), jnp.bfloat16),
            pltpu.VMEM((rows, MIX_WIDTH), jnp.bfloat16)]


def _block_rows(n_rows):
    rb = min(MAX_BLOCK_ROWS, n_rows)
    assert n_rows % rb == 0 and rb % SUBLANES == 0
    return rb, rb // SUBLANES


def _lanes(slab):
    return slice(slab * LANES, (slab + 1) * LANES)


def _mix_inputs(h, w, hn_ref):
    hn_ref[...] = _rmsnorm(h, w.norm[...]).astype(jnp.bfloat16)
    nc = CONV_CH // WEIGHT_WINDOW
    u_conv = []
    for c in range(nc):
        a = _dot(hn_ref[...], w.w_in[c][...])
        gate = _dot(hn_ref[...], w.w_in[nc + c][...])
        u_conv.append(a * jax.nn.sigmoid(gate))
    u_pool = [_dot(hn_ref[...], w_ref[...]) for w_ref in w.w_in[2 * nc:]]
    return u_conv, u_pool


def _slab(chunks, s, rows=slice(None)):
    per = WEIGHT_WINDOW // LANES
    return chunks[s // per][rows, _lanes(s % per)]


def _causal_dwconv(ext_c, w, y_ref, n_streams, n_rows):
    rb, stride = _block_rows(n_rows)
    for s in range(SLABS):
        bias = w.conv_b[s]
        taps = [w.conv_w[s, k] for k in range(CONV_K)]
        for b in range(n_streams):
            for r0 in range(0, n_rows, rb):
                acc = [bias] * stride
                for m in range(stride - 1 + CONV_K):
                    x = ext_c[b * SLABS + s, pl.ds(CONV_OFF + r0 + m, SUBLANES, stride=stride), :]
                    for j in range(stride):
                        k = m - j
                        if 0 <= k < CONV_K:
                            acc[j] = acc[j] + taps[k] * x
                for j in range(stride):
                    y_ref[s, pl.ds(b * n_rows + r0 + j, SUBLANES, stride=stride), :] = acc[j]


def _pool(ext_p, p_ref, n_streams, n_rows):
    rb, stride = _block_rows(n_rows)
    for grp, window in enumerate(POOL_WINDOWS):
        for s in range(grp * SLABS_PER_GROUP, (grp + 1) * SLABS_PER_GROUP):
            for b in range(n_streams):
                for r0 in range(0, n_rows, rb):
                    for j in range(stride):
                        base = POOL_PAD + r0 + j
                        tok = ext_p[b * SLABS + s, pl.ds(base, SUBLANES, stride=stride), :]
                        tot = tok
                        for i in range(1, window):
                            tot = tot + ext_p[b * SLABS + s, pl.ds(base - i, SUBLANES, stride=stride), :]
                        p_ref[s, pl.ds(b * n_rows + r0 + j, SUBLANES, stride=stride), :] = (
                            tot * (1.0 / window) - tok)


def _mix_tail(h, w, sc, n_streams, n_rows):
    _causal_dwconv(sc.ext_c, w, sc.y, n_streams, n_rows)
    _pool(sc.ext_p, sc.p, n_streams, n_rows)
    y = jnp.concatenate([sc.y[s] for s in range(SLABS)], axis=-1)
    cn = _rmsnorm(y, w.conv_norm[...])
    sc.cp[:, :CONV_CH] = (cn * jax.nn.sigmoid(cn)).astype(jnp.bfloat16)
    for grp in range(len(POOL_WINDOWS)):
        lanes = slice(grp * POOL_GROUP, (grp + 1) * POOL_GROUP)
        slabs = range(grp * SLABS_PER_GROUP, (grp + 1) * SLABS_PER_GROUP)
        p = jnp.concatenate([sc.p[s] for s in slabs], axis=-1).astype(jnp.bfloat16)
        sc.cp[:, CONV_CH + grp * POOL_GROUP:CONV_CH + (grp + 1) * POOL_GROUP] = (
            _dot(p, w.pool_w[grp]) * w.pool_scale[:, lanes]).astype(jnp.bfloat16)
    return jnp.concatenate(
        [h[:, c * WEIGHT_WINDOW:(c + 1) * WEIGHT_WINDOW] + _dot(sc.cp[...], w_ref[...])
         for c, w_ref in enumerate(w.w_out)], axis=-1)


N_MIX_SCRATCH = len(_MixScratch._fields)


def _mix_prompt_kernel(h_ref, hc_ref, hp_ref, *refs):
    w = _mix_weight_refs(refs[:N_MIX_WEIGHT_REFS])
    o_ref, ct_ref, pt_ref = refs[N_MIX_WEIGHT_REFS:-N_MIX_SCRATCH]
    sc = _MixScratch(*refs[-N_MIX_SCRATCH:])
    i = pl.program_id(0)
    tm = h_ref.shape[0]

    @pl.when(i == 0)
    def _():
        sc.ext_c[:, 0:CONV_PAD, :] = hc_ref[...]
        sc.ext_p[:, 0:POOL_PAD, :] = hp_ref[...]

    @pl.when(i > 0)
    def _():
        sc.ext_c[:, 0:CONV_PAD, :] = sc.ext_c[:, tm:tm + CONV_PAD, :]
        sc.ext_p[:, 0:POOL_PAD, :] = sc.ext_p[:, tm:tm + POOL_PAD, :]

    h = h_ref[...]
    u_conv, u_pool = _mix_inputs(h, w, sc.hn)
    for s in range(SLABS):
        sc.ext_c[s, CONV_PAD:, :] = _slab(u_conv, s)
        sc.ext_p[s, POOL_PAD:, :] = _slab(u_pool, s)
    o_ref[...] = _mix_tail(h, w, sc, 1, tm)
    ct_ref[...] = sc.ext_c[:, tm:tm + CONV_PAD, :]
    pt_ref[...] = sc.ext_p[:, tm:tm + POOL_PAD, :]


def _mix_small_kernel(h_ref, hm_ref, sc_ref, sp_ref, *refs):
    w = _mix_weight_refs(refs[:N_MIX_WEIGHT_REFS])
    o_ref, nc_ref, np_ref, mc_ref, mp_ref = refs[N_MIX_WEIGHT_REFS:-N_MIX_SCRATCH]
    sc = _MixScratch(*refs[-N_MIX_SCRATCH:])
    n_streams = sc_ref.shape[0]
    n_rows = sc.ext_c.shape[1] - CONV_PAD

    @pl.when(pl.program_id(0) == 0)
    def _():
        m_conv, m_pool = _mix_inputs(hm_ref[...], w, sc.hn.at[pl.ds(0, N_META)])
        for s in range(SLABS):
            mc_ref[s] = _slab(m_conv, s)
            mp_ref[s] = _slab(m_pool, s)

    h = h_ref[...]
    u_conv, u_pool = _mix_inputs(h, w, sc.hn)
    for s in range(SLABS):
        for b in range(n_streams):
            rows = slice(b * n_rows, (b + 1) * n_rows)
            sc.ext_c[b * SLABS + s, CONV_OFF:CONV_PAD, :] = sc_ref[b, :, _lanes(s)]
            sc.ext_p[b * SLABS + s, POOL_OFF:POOL_PAD, :] = sp_ref[b, :, _lanes(s)]
            sc.ext_c[b * SLABS + s, CONV_PAD:, :] = _slab(u_conv, s, rows)
            sc.ext_p[b * SLABS + s, POOL_PAD:, :] = _slab(u_pool, s, rows)

    o_ref[...] = _mix_tail(h, w, sc, n_streams, n_rows)
    for s in range(SLABS):
        for b in range(n_streams):
            nc_ref[b, :, _lanes(s)] = sc.ext_c[b * SLABS + s, n_rows + CONV_OFF:, :]
            np_ref[b, :, _lanes(s)] = sc.ext_p[b * SLABS + s, n_rows + POOL_OFF:, :]


def _const_spec(shape):
    zeros = (0,) * len(shape)
    return pl.BlockSpec(shape, lambda *_: zeros, pipeline_mode=pl.Buffered(1))


def _mix_prompt(h, hist_c, hist_p, weights):
    t, d = h.shape
    tm = MIX_ROWS
    assert t % tm == 0
    w_args, w_specs = _mix_weight_operands(weights)
    return pl.pallas_call(
        _mix_prompt_kernel,
        out_shape=(jax.ShapeDtypeStruct((t, d), jnp.float32),
                   jax.ShapeDtypeStruct((SLABS, CONV_PAD, LANES), jnp.float32),
                   jax.ShapeDtypeStruct((SLABS, POOL_PAD, LANES), jnp.float32)),
        grid=(t // tm,),
        in_specs=[pl.BlockSpec((tm, d), lambda i: (i, 0)),
                  _const_spec((SLABS, CONV_PAD, LANES)),
                  _const_spec((SLABS, POOL_PAD, LANES))] + w_specs,
        out_specs=(pl.BlockSpec((tm, d), lambda i: (i, 0)),
                   pl.BlockSpec((SLABS, CONV_PAD, LANES), lambda i: (0, 0, 0)),
                   pl.BlockSpec((SLABS, POOL_PAD, LANES), lambda i: (0, 0, 0))),
        scratch_shapes=_mix_scratch_shapes(1, tm),
        compiler_params=pltpu.CompilerParams(
            dimension_semantics=("arbitrary",),
            vmem_limit_bytes=VMEM_LIMIT_BYTES),
        name="mix_prompt",
    )(h, hist_c, hist_p, *w_args)


def _mix_small(h, state_c, state_p, weights):
    t, d = h.shape
    n_streams, _, _ = state_c.shape
    n_tok = t - N_META
    n_rows = n_tok // n_streams
    grp = MIX_SMALL_STREAMS
    tm = grp * n_rows
    assert n_rows * n_streams == n_tok and n_rows % SUBLANES == 0
    assert n_streams % grp == 0 and n_tok % N_META == 0 and tm >= N_META
    w_args, w_specs = _mix_weight_operands(weights)
    return pl.pallas_call(
        _mix_small_kernel,
        out_shape=(jax.ShapeDtypeStruct((n_tok, d), jnp.float32),
                   jax.ShapeDtypeStruct((n_streams, CONV_HIST, CONV_CH), jnp.float32),
                   jax.ShapeDtypeStruct((n_streams, POOL_HIST, POOL_CH), jnp.float32),
                   jax.ShapeDtypeStruct((SLABS, N_META, LANES), jnp.float32),
                   jax.ShapeDtypeStruct((SLABS, N_META, LANES), jnp.float32)),
        grid=(n_streams // grp,),
        in_specs=[pl.BlockSpec((tm, d), lambda i: (i, 0)),
                  pl.BlockSpec((N_META, d), lambda i: (n_tok // N_META, 0), pipeline_mode=pl.Buffered(1)),
                  pl.BlockSpec((grp, CONV_HIST, CONV_CH), lambda i: (i, 0, 0)),
                  pl.BlockSpec((grp, POOL_HIST, POOL_CH), lambda i: (i, 0, 0))]
        + w_specs,
        out_specs=(pl.BlockSpec((tm, d), lambda i: (i, 0)),
                   pl.BlockSpec((grp, CONV_HIST, CONV_CH), lambda i: (i, 0, 0)),
                   pl.BlockSpec((grp, POOL_HIST, POOL_CH), lambda i: (i, 0, 0)),
                   pl.BlockSpec((SLABS, N_META, LANES), lambda i: (0, 0, 0)),
                   pl.BlockSpec((SLABS, N_META, LANES), lambda i: (0, 0, 0))),
        scratch_shapes=_mix_scratch_shapes(grp, n_rows),
        compiler_params=pltpu.CompilerParams(
            dimension_semantics=("arbitrary",),
            vmem_limit_bytes=VMEM_LIMIT_BYTES),
        name="mix_small",
    )(h, h, state_c, state_p, *w_args)


def _to_slabs(v):
    return v.reshape(v.shape[0], SLABS, LANES).transpose(1, 0, 2)


def _from_slabs(v):
    return v.transpose(1, 0, 2).reshape(v.shape[1], SLABS * LANES)


def kernel(x_prompt, x_sample, state_conv, state_pool, meta_tokens, ffn1_norm, ffn1_w_gate, ffn1_w_up, ffn1_w_down, mix_norm, w_in, conv_w, conv_b, conv_norm, pool_w, pool_scale, w_out, ffn2_norm, ffn2_w_gate, ffn2_w_up, ffn2_w_down, final_norm):
    depth = ffn1_norm.shape[0]
    b_p, seq, d = x_prompt.shape
    b_s, seq_s, _ = x_sample.shape
    assert depth == 1 and b_p == 1
    bf = jnp.bfloat16

    xp = x_prompt.reshape(seq, d)
    xs = jnp.concatenate([x_sample.reshape(b_s * seq_s, d), meta_tokens.astype(x_sample.dtype)], axis=0)

    row = lambda v: v.reshape(1, -1)
    fg = row(final_norm)
    hs, wgu1, wd1 = _ffn(xs, row(ffn1_norm[0]), (ffn1_w_gate[0], ffn1_w_up[0], ffn1_w_down[0]), fg,
                         final_norm=False, name="ffn1_small")
    hp, wd2, w_in_bf, w_out_bf, wgu2 = _ffn(
        xp, row(ffn1_norm[0]), (wgu1, wd1), fg,
        cast=(ffn2_w_down[0], w_in[0], w_out[0]), cast_pairs=((ffn2_w_gate[0], ffn2_w_up[0]),),
        final_norm=False, name="ffn1_prompt")
    ffn2 = (row(ffn2_norm[0]), (wgu2, wd2), fg)
    mixw = _MixWeights(
        norm=row(mix_norm[0]),
        w_in=w_in_bf,
        conv_w=jnp.broadcast_to(_to_slabs(conv_w[0])[:, :, None, :], (SLABS, CONV_K, SUBLANES, LANES)),
        conv_b=jnp.broadcast_to(_to_slabs(row(conv_b[0])), (SLABS, SUBLANES, LANES)),
        conv_norm=row(conv_norm[0]),
        pool_w=pool_w[0].astype(bf),
        pool_scale=row(pool_scale[0]),
        w_out=w_out_bf)

    hs, new_conv_s, new_pool_s, meta_c, meta_p = _mix_small(hs, state_conv[0], state_pool[0], mixw)
    hist_c = jnp.concatenate([jnp.zeros((SLABS, CONV_PAD - N_META, LANES), jnp.float32), meta_c], axis=1)
    hp, conv_tail, pool_tail = _mix_prompt(hp, hist_c, meta_p, mixw)

    ys = _ffn(hs, *ffn2, final_norm=True, name="ffn2_small")
    yp = _ffn(hp, *ffn2, final_norm=True, name="ffn2_prompt")

    y_prompt = yp.reshape(b_p, seq, d)
    y_sample = ys.reshape(b_s, seq_s, d)
    new_conv_prompt = _from_slabs(conv_tail[:, CONV_OFF:, :])[None, None]
    new_pool_prompt = _from_slabs(pool_tail[:, POOL_OFF:, :])[None, None]
    return (y_prompt, y_sample, new_conv_prompt, new_pool_prompt, new_conv_s[None], new_pool_s[None])
```

```python
import functools
from typing import Any, NamedTuple

import jax
import jax.numpy as jnp
from jax import lax
from jax.experimental import pallas as pl
from jax.experimental.pallas import tpu as pltpu

D_MODEL = 2048
N_META = 16
CONV_CH = 1024
POOL_CH = 1024
MIX_WIDTH = CONV_CH + POOL_CH
CONV_K = 31
CONV_HIST = CONV_K - 1
POOL_WINDOWS = (2, 4, 8, 16)
POOL_GROUP = POOL_CH // len(POOL_WINDOWS)
POOL_HIST = max(POOL_WINDOWS) - 1
EPS = 1e-6

SUBLANES = 8
BF16_SUBLANES = 16
LANES = 128
VMEM_LIMIT_BYTES = 60 * 1024 * 1024
VMEM_HEADROOM_BYTES = 10 * 1024 * 1024


def _vmem_limit(buffer_bytes):
    return min(VMEM_LIMIT_BYTES, buffer_bytes + VMEM_HEADROOM_BYTES)
WEIGHT_WINDOW = 512

SLABS = CONV_CH // LANES
assert POOL_CH == CONV_CH and POOL_GROUP % LANES == 0
SLABS_PER_GROUP = POOL_GROUP // LANES

CONV_PAD = 32
POOL_PAD = 16
CONV_OFF = CONV_PAD - CONV_HIST
POOL_OFF = POOL_PAD - POOL_HIST

FFN_ROWS = 1024
FFN_COLS = 512
FFN_COLS_F32 = 512
MIX_ROWS = 512
MIX_SMALL_STREAMS = 8
MAX_BLOCK_ROWS = 4 * SUBLANES


def _rmsnorm(x, g):
    ms = jnp.mean(x * x, axis=-1, keepdims=True)
    return x * lax.rsqrt(ms + EPS) * g


def _dot(a, b):
    return jnp.dot(a, b, preferred_element_type=jnp.float32)


def _ffn_kernel(x_hbm, g_ref, wg_ref, wu_ref, wd_ref, fg_ref, *refs, final_norm, n_cast, emit_weights):
    n_own = 3 if emit_weights else 0
    cast_in, o_ref = refs[:n_cast], refs[n_cast]
    xn_ref, x_buf, x_sem = refs[-3:]
    own_out, cast_out = refs[n_cast + 1:n_cast + 1 + n_own], refs[n_cast + 1 + n_own:-3]
    i, j = pl.program_id(0), pl.program_id(1)
    tm = x_buf.shape[0]

    def x_copy(tile):
        return pltpu.make_async_copy(x_hbm.at[pl.ds(tile * tm, tm), :], x_buf, x_sem)

    @pl.when(j == 0)
    def _():
        @pl.when(i == 0)
        def _():
            x_copy(0).start()

        x_copy(i).wait()
        x = x_buf[...]
        xn_ref[...] = _rmsnorm(x, g_ref[...]).astype(jnp.bfloat16)
        o_ref[...] = x

        @pl.when(i + 1 < pl.num_programs(0))
        def _():
            x_copy(i + 1).start()

    wg, wu, wd = wg_ref[...], wu_ref[...], wd_ref[...]
    if emit_weights:
        wg, wu, wd = (w.astype(jnp.bfloat16) for w in (wg, wu, wd))
        for dst, w in zip(own_out, (wg, wu, wd)):
            dst[...] = w

    xn = xn_ref[...]
    gate = _dot(xn, wg)
    up = _dot(xn, wu)
    hid = (0.5 * (gate * jax.nn.sigmoid(gate)) * up).astype(jnp.bfloat16)
    o_ref[...] += _dot(hid, wd)

    for src, dst in zip(cast_in, cast_out):
        dst[...] = src[...].astype(jnp.bfloat16)

    if final_norm:
        @pl.when(j == pl.num_programs(1) - 1)
        def _():
            o_ref[...] = _rmsnorm(o_ref[...], fg_ref[...])


def _cast_spec(shape, ni, nj):
    r, c = shape
    if r % ni == 0 and c % nj == 0 and (c // nj) % LANES == 0:
        return pl.BlockSpec((r // ni, c // nj), lambda i, j: (i, j))
    if r % nj == 0 and c % ni == 0 and (c // ni) % LANES == 0:
        return pl.BlockSpec((r // nj, c // ni), lambda i, j: (j, i))
    nb = max(n for n in range(1, nj + 1) if c % n == 0 and (c // n) % LANES == 0)
    assert r % ni == 0 and (r // ni) % BF16_SUBLANES == 0
    return pl.BlockSpec((r // ni, c // nb), lambda i, j: (i, jnp.minimum(j, nb - 1)))


def _ffn(x, g, wg, wu, wd, fg, cast=(), *, final_norm, name):
    t, d = x.shape
    f = wg.shape[1]
    emit_weights = wg.dtype == jnp.float32
    tm = FFN_ROWS if t % FFN_ROWS == 0 else t
    tf = FFN_COLS_F32 if emit_weights else FFN_COLS
    assert t % tm == 0 and f % tf == 0
    grid = (t // tm, f // tf)
    assert not emit_weights or grid[0] == 1
    w_specs = [pl.BlockSpec((d, tf), lambda i, j: (0, j)),
               pl.BlockSpec((d, tf), lambda i, j: (0, j)),
               pl.BlockSpec((tf, d), lambda i, j: (j, 0))]
    own = [wg, wu, wd] if emit_weights else []
    cast_specs = [_cast_spec(w.shape, *grid) for w in cast]
    weight_block = 3 * d * tf
    streamed = weight_block * wg.dtype.itemsize + (weight_block * 2 if emit_weights else 0)
    streamed += sum(spec.block_shape[0] * spec.block_shape[1] * (4 + 2) for spec in cast_specs)
    buffer_bytes = tm * d * (4 + 2 + 2 * 4) + 2 * streamed
    out = pl.pallas_call(
        functools.partial(_ffn_kernel, final_norm=final_norm, n_cast=len(cast), emit_weights=emit_weights),
        out_shape=[jax.ShapeDtypeStruct((t, d), jnp.float32)]
        + [jax.ShapeDtypeStruct(w.shape, jnp.bfloat16) for w in own + list(cast)],
        grid=grid,
        in_specs=[pl.BlockSpec(memory_space=pl.ANY),
                  pl.BlockSpec((1, d), lambda i, j: (0, 0))]
        + w_specs + [pl.BlockSpec((1, d), lambda i, j: (0, 0))] + cast_specs,
        out_specs=[pl.BlockSpec((tm, d), lambda i, j: (i, 0))] + w_specs[:len(own)] + cast_specs,
        scratch_shapes=[pltpu.VMEM((tm, d), jnp.bfloat16),
                        pltpu.VMEM((tm, d), jnp.float32),
                        pltpu.SemaphoreType.DMA(())],
        compiler_params=pltpu.CompilerParams(
            dimension_semantics=("arbitrary", "arbitrary"),
            vmem_limit_bytes=_vmem_limit(buffer_bytes)),
        name=name,
    )(x, g, wg, wu, wd, fg, *cast)
    return out[0] if len(out) == 1 else out


class _MixWeights(NamedTuple):
    norm: Any
    w_in: Any
    conv_w: Any
    conv_b: Any
    conv_norm: Any
    pool_w: Any
    pool_scale: Any
    w_out: Any


def _mix_weight_operands(w):
    def windows(v):
        k, n = v.shape
        assert n % WEIGHT_WINDOW == 0
        return [(v, pl.BlockSpec((k, WEIGHT_WINDOW), functools.partial(lambda c, *_: (0, c), c),
                                 pipeline_mode=pl.Buffered(1))) for c in range(n // WEIGHT_WINDOW)]

    pairs = []
    for name, v in w._asdict().items():
        pairs += windows(v) if name in ("w_in", "w_out") else [(v, _const_spec(v.shape))]
    return [v for v, _ in pairs], [s for _, s in pairs]


N_IN_WINDOWS = (2 * CONV_CH + POOL_CH) // WEIGHT_WINDOW
N_OUT_WINDOWS = D_MODEL // WEIGHT_WINDOW
N_MIX_WEIGHT_REFS = len(_MixWeights._fields) - 2 + N_IN_WINDOWS + N_OUT_WINDOWS


def _mix_weight_refs(refs):
    refs = list(refs)
    assert len(refs) == N_MIX_WEIGHT_REFS
    take = lambda n: tuple(refs.pop(0) for _ in range(n))
    return _MixWeights(**{name: take(N_IN_WINDOWS) if name == "w_in" else
                          take(N_OUT_WINDOWS) if name == "w_out" else refs.pop(0)
                          for name in _MixWeights._fields})


class _MixScratch(NamedTuple):
    ext_c: Any
    ext_p: Any
    y: Any
    p: Any
    hn: Any
    cp: Any


def _mix_scratch_shapes(n_streams, n_rows):
    rows = n_streams * n_rows
    return [pltpu.VMEM((n_streams * SLABS, CONV_PAD + n_rows, LANES), jnp.float32),
            pltpu.VMEM((n_streams * SLABS, POOL_PAD + n_rows, LANES), jnp.float32),
            pltpu.VMEM((SLABS, rows, LANES), jnp.float32),
            pltpu.VMEM((SLABS, rows, LANES), jnp.float32),
            pltpu.VMEM((rows, D_MODEL), jnp.bfloat16),
            pltpu.VMEM((rows, MIX_WIDTH), jnp.bfloat16)]


def _block_rows(n_rows):
    rb = min(MAX_BLOCK_ROWS, n_rows)
    assert n_rows % rb == 0 and rb % SUBLANES == 0
    return rb, rb // SUBLANES


def _lanes(slab):
    return slice(slab * LANES, (slab + 1) * LANES)


def _mix_inputs(h, w, hn_ref):
    hn_ref[...] = _rmsnorm(h, w.norm[...]).astype(jnp.bfloat16)
    nc = CONV_CH // WEIGHT_WINDOW
    u_conv = []
    for c in range(nc):
        a = _dot(hn_ref[...], w.w_in[c][...])
        gate = _dot(hn_ref[...], w.w_in[nc + c][...])
        u_conv.append(a * jax.nn.sigmoid(gate))
    u_pool = [_dot(hn_ref[...], w_ref[...]) for w_ref in w.w_in[2 * nc:]]
    return u_conv, u_pool


def _slab(chunks, s, rows=slice(None)):
    per = WEIGHT_WINDOW // LANES
    return chunks[s // per][rows, _lanes(s % per)]


def _causal_dwconv(ext_c, w, y_ref, n_streams, n_rows):
    rb, stride = _block_rows(n_rows)
    for s in range(SLABS):
        bias = w.conv_b[s]
        taps = [w.conv_w[s, k] for k in range(CONV_K)]
        for b in range(n_streams):
            for r0 in range(0, n_rows, rb):
                acc = [bias] * stride
                for m in range(stride - 1 + CONV_K):
                    x = ext_c[b * SLABS + s, pl.ds(CONV_OFF + r0 + m, SUBLANES, stride=stride), :]
                    for j in range(stride):
                        k = m - j
                        if 0 <= k < CONV_K:
                            acc[j] = acc[j] + taps[k] * x
                for j in range(stride):
                    y_ref[s, pl.ds(b * n_rows + r0 + j, SUBLANES, stride=stride), :] = acc[j]


def _pool(ext_p, p_ref, n_streams, n_rows):
    rb, stride = _block_rows(n_rows)
    for grp, window in enumerate(POOL_WINDOWS):
        for s in range(grp * SLABS_PER_GROUP, (grp + 1) * SLABS_PER_GROUP):
            for b in range(n_streams):
                for r0 in range(0, n_rows, rb):
                    for j in range(stride):
                        base = POOL_PAD + r0 + j
                        tok = ext_p[b * SLABS + s, pl.ds(base, SUBLANES, stride=stride), :]
                        tot = tok
                        for i in range(1, window):
                            tot = tot + ext_p[b * SLABS + s, pl.ds(base - i, SUBLANES, stride=stride), :]
                        p_ref[s, pl.ds(b * n_rows + r0 + j, SUBLANES, stride=stride), :] = (
                            tot * (1.0 / window) - tok)


def _mix_tail(h, w, sc, n_streams, n_rows):
    _causal_dwconv(sc.ext_c, w, sc.y, n_streams, n_rows)
    _pool(sc.ext_p, sc.p, n_streams, n_rows)
    y = jnp.concatenate([sc.y[s] for s in range(SLABS)], axis=-1)
    cn = _rmsnorm(y, w.conv_norm[...])
    sc.cp[:, :CONV_CH] = (cn * jax.nn.sigmoid(cn)).astype(jnp.bfloat16)
    for grp in range(len(POOL_WINDOWS)):
        lanes = slice(grp * POOL_GROUP, (grp + 1) * POOL_GROUP)
        slabs = range(grp * SLABS_PER_GROUP, (grp + 1) * SLABS_PER_GROUP)
        p = jnp.concatenate([sc.p[s] for s in slabs], axis=-1).astype(jnp.bfloat16)
        sc.cp[:, CONV_CH + grp * POOL_GROUP:CONV_CH + (grp + 1) * POOL_GROUP] = (
            _dot(p, w.pool_w[grp]) * w.pool_scale[:, lanes]).astype(jnp.bfloat16)
    return jnp.concatenate(
        [h[:, c * WEIGHT_WINDOW:(c + 1) * WEIGHT_WINDOW] + _dot(sc.cp[...], w_ref[...])
         for c, w_ref in enumerate(w.w_out)], axis=-1)


N_MIX_SCRATCH = len(_MixScratch._fields)


def _mix_prompt_kernel(h_ref, hc_ref, hp_ref, *refs):
    w = _mix_weight_refs(refs[:N_MIX_WEIGHT_REFS])
    o_ref, ct_ref, pt_ref = refs[N_MIX_WEIGHT_REFS:-N_MIX_SCRATCH]
    sc = _MixScratch(*refs[-N_MIX_SCRATCH:])
    i = pl.program_id(0)
    tm = h_ref.shape[0]

    @pl.when(i == 0)
    def _():
        sc.ext_c[:, 0:CONV_PAD, :] = hc_ref[...]
        sc.ext_p[:, 0:POOL_PAD, :] = hp_ref[...]

    @pl.when(i > 0)
    def _():
        sc.ext_c[:, 0:CONV_PAD, :] = sc.ext_c[:, tm:tm + CONV_PAD, :]
        sc.ext_p[:, 0:POOL_PAD, :] = sc.ext_p[:, tm:tm + POOL_PAD, :]

    h = h_ref[...]
    u_conv, u_pool = _mix_inputs(h, w, sc.hn)
    for s in range(SLABS):
        sc.ext_c[s, CONV_PAD:, :] = _slab(u_conv, s)
        sc.ext_p[s, POOL_PAD:, :] = _slab(u_pool, s)
    o_ref[...] = _mix_tail(h, w, sc, 1, tm)
    ct_ref[...] = sc.ext_c[:, tm:tm + CONV_PAD, :]
    pt_ref[...] = sc.ext_p[:, tm:tm + POOL_PAD, :]


def _mix_small_kernel(h_ref, hm_ref, sc_ref, sp_ref, *refs):
    w = _mix_weight_refs(refs[:N_MIX_WEIGHT_REFS])
    o_ref, nc_ref, np_ref, mc_ref, mp_ref = refs[N_MIX_WEIGHT_REFS:-N_MIX_SCRATCH]
    sc = _MixScratch(*refs[-N_MIX_SCRATCH:])
    n_streams = sc_ref.shape[0]
    n_rows = sc.ext_c.shape[1] - CONV_PAD

    @pl.when(pl.program_id(0) == 0)
    def _():
        m_conv, m_pool = _mix_inputs(hm_ref[...], w, sc.hn.at[pl.ds(0, N_META)])
        for s in range(SLABS):
            mc_ref[s] = _slab(m_conv, s)
            mp_ref[s] = _slab(m_pool, s)

    h = h_ref[...]
    u_conv, u_pool = _mix_inputs(h, w, sc.hn)
    for s in range(SLABS):
        for b in range(n_streams):
            rows = slice(b * n_rows, (b + 1) * n_rows)
            sc.ext_c[b * SLABS + s, CONV_OFF:CONV_PAD, :] = sc_ref[b, :, _lanes(s)]
            sc.ext_p[b * SLABS + s, POOL_OFF:POOL_PAD, :] = sp_ref[b, :, _lanes(s)]
            sc.ext_c[b * SLABS + s, CONV_PAD:, :] = _slab(u_conv, s, rows)
            sc.ext_p[b * SLABS + s, POOL_PAD:, :] = _slab(u_pool, s, rows)

    o_ref[...] = _mix_tail(h, w, sc, n_streams, n_rows)
    for s in range(SLABS):
        for b in range(n_streams):
            nc_ref[b, :, _lanes(s)] = sc.ext_c[b * SLABS + s, n_rows + CONV_OFF:, :]
            np_ref[b, :, _lanes(s)] = sc.ext_p[b * SLABS + s, n_rows + POOL_OFF:, :]


def _const_spec(shape):
    zeros = (0,) * len(shape)
    return pl.BlockSpec(shape, lambda *_: zeros, pipeline_mode=pl.Buffered(1))


def _mix_prompt(h, hist_c, hist_p, weights):
    t, d = h.shape
    tm = MIX_ROWS
    assert t % tm == 0
    w_args, w_specs = _mix_weight_operands(weights)
    return pl.pallas_call(
        _mix_prompt_kernel,
        out_shape=(jax.ShapeDtypeStruct((t, d), jnp.float32),
                   jax.ShapeDtypeStruct((SLABS, CONV_PAD, LANES), jnp.float32),
                   jax.ShapeDtypeStruct((SLABS, POOL_PAD, LANES), jnp.float32)),
        grid=(t // tm,),
        in_specs=[pl.BlockSpec((tm, d), lambda i: (i, 0)),
                  _const_spec((SLABS, CONV_PAD, LANES)),
                  _const_spec((SLABS, POOL_PAD, LANES))] + w_specs,
        out_specs=(pl.BlockSpec((tm, d), lambda i: (i, 0)),
                   pl.BlockSpec((SLABS, CONV_PAD, LANES), lambda i: (0, 0, 0)),
                   pl.BlockSpec((SLABS, POOL_PAD, LANES), lambda i: (0, 0, 0))),
        scratch_shapes=_mix_scratch_shapes(1, tm),
        compiler_params=pltpu.CompilerParams(
            dimension_semantics=("arbitrary",),
            vmem_limit_bytes=VMEM_LIMIT_BYTES),
        name="mix_prompt",
    )(h, hist_c, hist_p, *w_args)


def _mix_small(h, state_c, state_p, weights):
    t, d = h.shape
    n_streams, _, _ = state_c.shape
    n_tok = t - N_META
    n_rows = n_tok // n_streams
    grp = MIX_SMALL_STREAMS
    tm = grp * n_rows
    assert n_rows * n_streams == n_tok and n_rows % SUBLANES == 0
    assert n_streams % grp == 0 and n_tok % N_META == 0 and tm >= N_META
    w_args, w_specs = _mix_weight_operands(weights)
    return pl.pallas_call(
        _mix_small_kernel,
        out_shape=(jax.ShapeDtypeStruct((n_tok, d), jnp.float32),
                   jax.ShapeDtypeStruct((n_streams, CONV_HIST, CONV_CH), jnp.float32),
                   jax.ShapeDtypeStruct((n_streams, POOL_HIST, POOL_CH), jnp.float32),
                   jax.ShapeDtypeStruct((SLABS, N_META, LANES), jnp.float32),
                   jax.ShapeDtypeStruct((SLABS, N_META, LANES), jnp.float32)),
        grid=(n_streams // grp,),
        in_specs=[pl.BlockSpec((tm, d), lambda i: (i, 0)),
                  pl.BlockSpec((N_META, d), lambda i: (n_tok // N_META, 0), pipeline_mode=pl.Buffered(1)),
                  pl.BlockSpec((grp, CONV_HIST, CONV_CH), lambda i: (i, 0, 0)),
                  pl.BlockSpec((grp, POOL_HIST, POOL_CH), lambda i: (i, 0, 0))]
        + w_specs,
        out_specs=(pl.BlockSpec((tm, d), lambda i: (i, 0)),
                   pl.BlockSpec((grp, CONV_HIST, CONV_CH), lambda i: (i, 0, 0)),
                   pl.BlockSpec((grp, POOL_HIST, POOL_CH), lambda i: (i, 0, 0)),
                   pl.BlockSpec((SLABS, N_META, LANES), lambda i: (0, 0, 0)),
                   pl.BlockSpec((SLABS, N_META, LANES), lambda i: (0, 0, 0))),
        scratch_shapes=_mix_scratch_shapes(grp, n_rows),
        compiler_params=pltpu.CompilerParams(
            dimension_semantics=("arbitrary",),
            vmem_limit_bytes=VMEM_LIMIT_BYTES),
        name="mix_small",
    )(h, h, state_c, state_p, *w_args)


def _to_slabs(v):
    return v.reshape(v.shape[0], SLABS, LANES).transpose(1, 0, 2)


def _from_slabs(v):
    return v.transpose(1, 0, 2).reshape(v.shape[1], SLABS * LANES)


def kernel(x_prompt, x_sample, state_conv, state_pool, meta_tokens, ffn1_norm, ffn1_w_gate, ffn1_w_up, ffn1_w_down, mix_norm, w_in, conv_w, conv_b, conv_norm, pool_w, pool_scale, w_out, ffn2_norm, ffn2_w_gate, ffn2_w_up, ffn2_w_down, final_norm):
    depth = ffn1_norm.shape[0]
    b_p, seq, d = x_prompt.shape
    b_s, seq_s, _ = x_sample.shape
    assert depth == 1 and b_p == 1
    bf = jnp.bfloat16

    xp = x_prompt.reshape(seq, d)
    xs = jnp.concatenate([x_sample.reshape(b_s * seq_s, d), meta_tokens.astype(x_sample.dtype)], axis=0)

    row = lambda v: v.reshape(1, -1)
    fg = row(final_norm)
    hs, wg1, wu1, wd1 = _ffn(xs, row(ffn1_norm[0]), ffn1_w_gate[0], ffn1_w_up[0], ffn1_w_down[0], fg,
                             final_norm=False, name="ffn1_small")
    hp, wg2, wu2, wd2, w_in_bf, w_out_bf = _ffn(
        xp, row(ffn1_norm[0]), wg1, wu1, wd1, fg,
        cast=(ffn2_w_gate[0], ffn2_w_up[0], ffn2_w_down[0], w_in[0], w_out[0]),
        final_norm=False, name="ffn1_prompt")
    ffn2 = (row(ffn2_norm[0]), wg2, wu2, wd2, fg)
    mixw = _MixWeights(
        norm=row(mix_norm[0]),
        w_in=w_in_bf,
        conv_w=jnp.broadcast_to(_to_slabs(conv_w[0])[:, :, None, :], (SLABS, CONV_K, SUBLANES, LANES)),
        conv_b=jnp.broadcast_to(_to_slabs(row(conv_b[0])), (SLABS, SUBLANES, LANES)),
        conv_norm=row(conv_norm[0]),
        pool_w=pool_w[0].astype(bf),
        pool_scale=row(pool_scale[0]),
        w_out=w_out_bf)

    hs, new_conv_s, new_pool_s, meta_c, meta_p = _mix_small(hs, state_conv[0], state_pool[0], mixw)
    hist_c = jnp.concatenate([jnp.zeros((SLABS, CONV_PAD - N_META, LANES), jnp.float32), meta_c], axis=1)
    hp, conv_tail, pool_tail = _mix_prompt(hp, hist_c, meta_p, mixw)

    ys = _ffn(hs, *ffn2, final_norm=True, name="ffn2_small")
    yp = _ffn(hp, *ffn2, final_norm=True, name="ffn2_prompt")

    y_prompt = yp.reshape(b_p, seq, d)
    y_sample = ys.reshape(b_s, seq_s, d)
    new_conv_prompt = _from_slabs(conv_tail[:, CONV_OFF:, :])[None, None]
    new_pool_prompt = _from_slabs(pool_tail[:, POOL_OFF:, :])[None, None]
    return (y_prompt, y_sample, new_conv_prompt, new_pool_prompt, new_conv_s[None], new_pool_s[None])
```

```python
import functools
from typing import Any, NamedTuple

import jax
import jax.numpy as jnp
from jax import lax
from jax.experimental import pallas as pl
from jax.experimental.pallas import tpu as pltpu

D_MODEL = 2048
N_META = 16
CONV_CH = 1024
POOL_CH = 1024
MIX_WIDTH = CONV_CH + POOL_CH
CONV_K = 31
CONV_HIST = CONV_K - 1
POOL_WINDOWS = (2, 4, 8, 16)
POOL_GROUP = POOL_CH // len(POOL_WINDOWS)
POOL_HIST = max(POOL_WINDOWS) - 1
EPS = 1e-6

SUBLANES = 8
BF16_SUBLANES = 16
LANES = 128
VMEM_LIMIT_BYTES = 60 * 1024 * 1024
VMEM_HEADROOM_BYTES = 10 * 1024 * 1024


def _vmem_limit(buffer_bytes):
    return min(VMEM_LIMIT_BYTES, buffer_bytes + VMEM_HEADROOM_BYTES)
WEIGHT_WINDOW = 512

SLABS = CONV_CH // LANES
assert POOL_CH == CONV_CH and POOL_GROUP % LANES == 0
SLABS_PER_GROUP = POOL_GROUP // LANES

CONV_PAD = 32
POOL_PAD = 16
CONV_OFF = CONV_PAD - CONV_HIST
POOL_OFF = POOL_PAD - POOL_HIST

FFN_ROWS = 1024
FFN_COLS = 512
FFN_COLS_F32 = 512
MIX_ROWS = 512
MIX_SMALL_STREAMS = 8
MAX_BLOCK_ROWS = 4 * SUBLANES


def _rmsnorm(x, g):
    ms = jnp.mean(x * x, axis=-1, keepdims=True)
    return x * lax.rsqrt(ms + EPS) * g


def _dot(a, b):
    return jnp.dot(a, b, preferred_element_type=jnp.float32)


def _ffn_kernel(x_hbm, g_ref, wg_ref, wu_ref, wd_ref, fg_ref, *refs, final_norm, n_cast, emit_weights):
    n_own = 3 if emit_weights else 0
    cast_in, o_ref = refs[:n_cast], refs[n_cast]
    xn_ref, x_buf, x_sem = refs[-3:]
    own_out, cast_out = refs[n_cast + 1:n_cast + 1 + n_own], refs[n_cast + 1 + n_own:-3]
    i, j = pl.program_id(0), pl.program_id(1)
    tm = x_buf.shape[0]

    def x_copy(tile):
        return pltpu.make_async_copy(x_hbm.at[pl.ds(tile * tm, tm), :], x_buf, x_sem)

    @pl.when(j == 0)
    def _():
        @pl.when(i == 0)
        def _():
            x_copy(0).start()

        x_copy(i).wait()
        x = x_buf[...]
        xn_ref[...] = _rmsnorm(x, g_ref[...]).astype(jnp.bfloat16)
        o_ref[...] = x

        @pl.when(i + 1 < pl.num_programs(0))
        def _():
            x_copy(i + 1).start()

    wg, wu, wd = wg_ref[...], wu_ref[...], wd_ref[...]
    if emit_weights:
        wg, wu, wd = (w.astype(jnp.bfloat16) for w in (wg, wu, wd))
        for dst, w in zip(own_out, (wg, wu, wd)):
            dst[...] = w

    xn = xn_ref[...]
    gate = _dot(xn, wg)
    up = _dot(xn, wu)
    hid = (0.5 * (gate * jax.nn.sigmoid(gate)) * up).astype(jnp.bfloat16)
    o_ref[...] += _dot(hid, wd)

    for src, dst in zip(cast_in, cast_out):
        dst[...] = src[...].astype(jnp.bfloat16)

    if final_norm:
        @pl.when(j == pl.num_programs(1) - 1)
        def _():
            o_ref[...] = _rmsnorm(o_ref[...], fg_ref[...])


def _cast_spec(shape, ni, nj):
    r, c = shape
    if r % ni == 0 and c % nj == 0 and (c // nj) % LANES == 0:
        return pl.BlockSpec((r // ni, c // nj), lambda i, j: (i, j))
    if r % nj == 0 and c % ni == 0 and (c // ni) % LANES == 0:
        return pl.BlockSpec((r // nj, c // ni), lambda i, j: (j, i))
    nb = max(n for n in range(1, nj + 1) if c % n == 0 and (c // n) % LANES == 0)
    assert r % ni == 0 and (r // ni) % BF16_SUBLANES == 0
    return pl.BlockSpec((r // ni, c // nb), lambda i, j: (i, jnp.minimum(j, nb - 1)))


def _ffn(x, g, wg, wu, wd, fg, cast=(), *, final_norm, name):
    t, d = x.shape
    f = wg.shape[1]
    emit_weights = wg.dtype == jnp.float32
    tm = FFN_ROWS if t % FFN_ROWS == 0 else t
    tf = FFN_COLS_F32 if emit_weights else FFN_COLS
    assert t % tm == 0 and f % tf == 0
    grid = (t // tm, f // tf)
    assert not emit_weights or grid[0] == 1
    w_specs = [pl.BlockSpec((d, tf), lambda i, j: (0, j)),
               pl.BlockSpec((d, tf), lambda i, j: (0, j)),
               pl.BlockSpec((tf, d), lambda i, j: (j, 0))]
    own = [wg, wu, wd] if emit_weights else []
    cast_specs = [_cast_spec(w.shape, *grid) for w in cast]
    weight_block = 3 * d * tf
    streamed = weight_block * wg.dtype.itemsize + (weight_block * 2 if emit_weights else 0)
    streamed += sum(spec.block_shape[0] * spec.block_shape[1] * (4 + 2) for spec in cast_specs)
    buffer_bytes = tm * d * (4 + 2 + 2 * 4) + 2 * streamed
    out = pl.pallas_call(
        functools.partial(_ffn_kernel, final_norm=final_norm, n_cast=len(cast), emit_weights=emit_weights),
        out_shape=[jax.ShapeDtypeStruct((t, d), jnp.float32)]
        + [jax.ShapeDtypeStruct(w.shape, jnp.bfloat16) for w in own + list(cast)],
        grid=grid,
        in_specs=[pl.BlockSpec(memory_space=pl.ANY),
                  pl.BlockSpec((1, d), lambda i, j: (0, 0))]
        + w_specs + [pl.BlockSpec((1, d), lambda i, j: (0, 0))] + cast_specs,
        out_specs=[pl.BlockSpec((tm, d), lambda i, j: (i, 0))] + w_specs[:len(own)] + cast_specs,
        scratch_shapes=[pltpu.VMEM((tm, d), jnp.bfloat16),
                        pltpu.VMEM((tm, d), jnp.float32),
                        pltpu.SemaphoreType.DMA(())],
        compiler_params=pltpu.CompilerParams(
            dimension_semantics=("arbitrary", "arbitrary"),
            vmem_limit_bytes=_vmem_limit(buffer_bytes)),
        name=name,
    )(x, g, wg, wu, wd, fg, *cast)
    return out[0] if len(out) == 1 else out


class _MixWeights(NamedTuple):
    norm: Any
    w_in: Any
    conv_w: Any
    conv_b: Any
    conv_norm: Any
    pool_w: Any
    pool_scale: Any
    w_out: Any


def _mix_weight_operands(w):
    def windows(v):
        k, n = v.shape
        assert n % WEIGHT_WINDOW == 0
        return [(v, pl.BlockSpec((k, WEIGHT_WINDOW), functools.partial(lambda c, *_: (0, c), c),
                                 pipeline_mode=pl.Buffered(1))) for c in range(n // WEIGHT_WINDOW)]

    pairs = []
    for name, v in w._asdict().items():
        pairs += windows(v) if name in ("w_in", "w_out") else [(v, _const_spec(v.shape))]
    return [v for v, _ in pairs], [s for _, s in pairs]


N_IN_WINDOWS = (2 * CONV_CH + POOL_CH) // WEIGHT_WINDOW
N_OUT_WINDOWS = D_MODEL // WEIGHT_WINDOW
N_MIX_WEIGHT_REFS = len(_MixWeights._fields) - 2 + N_IN_WINDOWS + N_OUT_WINDOWS


def _mix_weight_refs(refs):
    refs = list(refs)
    assert len(refs) == N_MIX_WEIGHT_REFS
    take = lambda n: tuple(refs.pop(0) for _ in range(n))
    return _MixWeights(**{name: take(N_IN_WINDOWS) if name == "w_in" else
                          take(N_OUT_WINDOWS) if name == "w_out" else refs.pop(0)
                          for name in _MixWeights._fields})


class _MixScratch(NamedTuple):
    ext_c: Any
    ext_p: Any
    y: Any
    p: Any
    hn: Any
    cp: Any


def _mix_scratch_shapes(n_streams, n_rows):
    rows = n_streams * n_rows
    return [pltpu.VMEM((n_streams * SLABS, CONV_PAD + n_rows, LANES), jnp.float32),
            pltpu.VMEM((n_streams * SLABS, POOL_PAD + n_rows, LANES), jnp.float32),
            pltpu.VMEM((SLABS, rows, LANES), jnp.float32),
            pltpu.VMEM((SLABS, rows, LANES), jnp.float32),
            pltpu.VMEM((rows, D_MODEL), jnp.bfloat16),
            pltpu.VMEM((rows, MIX_WIDTH), jnp.bfloat16)]


def _nbytes(shape, dtype):
    n = jnp.dtype(dtype).itemsize
    for dim in shape:
        n *= dim
    return n


def _mix_vmem_limit(w_args, w_specs, const_blocks, streamed_blocks, scratch):
    weights = sum(_nbytes(spec.block_shape, arg.dtype) for arg, spec in zip(w_args, w_specs))
    const = sum(_nbytes(shape, jnp.float32) for shape in const_blocks)
    streamed = sum(_nbytes(shape, jnp.float32) for shape in streamed_blocks)
    return _vmem_limit(weights + const + 2 * streamed + sum(_nbytes(s.shape, s.dtype) for s in scratch))


def _block_rows(n_rows):
    rb = min(MAX_BLOCK_ROWS, n_rows)
    assert n_rows % rb == 0 and rb % SUBLANES == 0
    return rb, rb // SUBLANES


def _lanes(slab):
    return slice(slab * LANES, (slab + 1) * LANES)


def _mix_inputs(h, w, hn_ref):
    hn_ref[...] = _rmsnorm(h, w.norm[...]).astype(jnp.bfloat16)
    nc = CONV_CH // WEIGHT_WINDOW
    u_conv = []
    for c in range(nc):
        a = _dot(hn_ref[...], w.w_in[c][...])
        gate = _dot(hn_ref[...], w.w_in[nc + c][...])
        u_conv.append(a * jax.nn.sigmoid(gate))
    u_pool = [_dot(hn_ref[...], w_ref[...]) for w_ref in w.w_in[2 * nc:]]
    return u_conv, u_pool


def _slab(chunks, s, rows=slice(None)):
    per = WEIGHT_WINDOW // LANES
    return chunks[s // per][rows, _lanes(s % per)]


def _causal_dwconv(ext_c, w, y_ref, n_streams, n_rows):
    rb, stride = _block_rows(n_rows)
    for s in range(SLABS):
        bias = w.conv_b[s]
        taps = [w.conv_w[s, k] for k in range(CONV_K)]
        for b in range(n_streams):
            for r0 in range(0, n_rows, rb):
                acc = [bias] * stride
                for m in range(stride - 1 + CONV_K):
                    x = ext_c[b * SLABS + s, pl.ds(CONV_OFF + r0 + m, SUBLANES, stride=stride), :]
                    for j in range(stride):
                        k = m - j
                        if 0 <= k < CONV_K:
                            acc[j] = acc[j] + taps[k] * x
                for j in range(stride):
                    y_ref[s, pl.ds(b * n_rows + r0 + j, SUBLANES, stride=stride), :] = acc[j]


def _pool(ext_p, p_ref, n_streams, n_rows):
    rb, stride = _block_rows(n_rows)
    for grp, window in enumerate(POOL_WINDOWS):
        for s in range(grp * SLABS_PER_GROUP, (grp + 1) * SLABS_PER_GROUP):
            for b in range(n_streams):
                for r0 in range(0, n_rows, rb):
                    for j in range(stride):
                        base = POOL_PAD + r0 + j
                        tok = ext_p[b * SLABS + s, pl.ds(base, SUBLANES, stride=stride), :]
                        tot = tok
                        for i in range(1, window):
                            tot = tot + ext_p[b * SLABS + s, pl.ds(base - i, SUBLANES, stride=stride), :]
                        p_ref[s, pl.ds(b * n_rows + r0 + j, SUBLANES, stride=stride), :] = (
                            tot * (1.0 / window) - tok)


def _mix_tail(h, w, sc, n_streams, n_rows):
    _causal_dwconv(sc.ext_c, w, sc.y, n_streams, n_rows)
    _pool(sc.ext_p, sc.p, n_streams, n_rows)
    y = jnp.concatenate([sc.y[s] for s in range(SLABS)], axis=-1)
    cn = _rmsnorm(y, w.conv_norm[...])
    sc.cp[:, :CONV_CH] = (cn * jax.nn.sigmoid(cn)).astype(jnp.bfloat16)
    for grp in range(len(POOL_WINDOWS)):
        lanes = slice(grp * POOL_GROUP, (grp + 1) * POOL_GROUP)
        slabs = range(grp * SLABS_PER_GROUP, (grp + 1) * SLABS_PER_GROUP)
        p = jnp.concatenate([sc.p[s] for s in slabs], axis=-1).astype(jnp.bfloat16)
        sc.cp[:, CONV_CH + grp * POOL_GROUP:CONV_CH + (grp + 1) * POOL_GROUP] = (
            _dot(p, w.pool_w[grp]) * w.pool_scale[:, lanes]).astype(jnp.bfloat16)
    return jnp.concatenate(
        [h[:, c * WEIGHT_WINDOW:(c + 1) * WEIGHT_WINDOW] + _dot(sc.cp[...], w_ref[...])
         for c, w_ref in enumerate(w.w_out)], axis=-1)


N_MIX_SCRATCH = len(_MixScratch._fields)


def _mix_prompt_kernel(h_ref, hc_ref, hp_ref, *refs):
    w = _mix_weight_refs(refs[:N_MIX_WEIGHT_REFS])
    o_ref, ct_ref, pt_ref = refs[N_MIX_WEIGHT_REFS:-N_MIX_SCRATCH]
    sc = _MixScratch(*refs[-N_MIX_SCRATCH:])
    i = pl.program_id(0)
    tm = h_ref.shape[0]

    @pl.when(i == 0)
    def _():
        sc.ext_c[:, 0:CONV_PAD, :] = hc_ref[...]
        sc.ext_p[:, 0:POOL_PAD, :] = hp_ref[...]

    @pl.when(i > 0)
    def _():
        sc.ext_c[:, 0:CONV_PAD, :] = sc.ext_c[:, tm:tm + CONV_PAD, :]
        sc.ext_p[:, 0:POOL_PAD, :] = sc.ext_p[:, tm:tm + POOL_PAD, :]

    h = h_ref[...]
    u_conv, u_pool = _mix_inputs(h, w, sc.hn)
    for s in range(SLABS):
        sc.ext_c[s, CONV_PAD:, :] = _slab(u_conv, s)
        sc.ext_p[s, POOL_PAD:, :] = _slab(u_pool, s)
    o_ref[...] = _mix_tail(h, w, sc, 1, tm)
    ct_ref[...] = sc.ext_c[:, tm:tm + CONV_PAD, :]
    pt_ref[...] = sc.ext_p[:, tm:tm + POOL_PAD, :]


def _mix_small_kernel(h_ref, hm_ref, sc_ref, sp_ref, *refs):
    w = _mix_weight_refs(refs[:N_MIX_WEIGHT_REFS])
    o_ref, nc_ref, np_ref, mc_ref, mp_ref = refs[N_MIX_WEIGHT_REFS:-N_MIX_SCRATCH]
    sc = _MixScratch(*refs[-N_MIX_SCRATCH:])
    n_streams = sc_ref.shape[0]
    n_rows = sc.ext_c.shape[1] - CONV_PAD

    @pl.when(pl.program_id(0) == 0)
    def _():
        m_conv, m_pool = _mix_inputs(hm_ref[...], w, sc.hn.at[pl.ds(0, N_META)])
        for s in range(SLABS):
            mc_ref[s] = _slab(m_conv, s)
            mp_ref[s] = _slab(m_pool, s)

    h = h_ref[...]
    u_conv, u_pool = _mix_inputs(h, w, sc.hn)
    for s in range(SLABS):
        for b in range(n_streams):
            rows = slice(b * n_rows, (b + 1) * n_rows)
            sc.ext_c[b * SLABS + s, CONV_OFF:CONV_PAD, :] = sc_ref[b, :, _lanes(s)]
            sc.ext_p[b * SLABS + s, POOL_OFF:POOL_PAD, :] = sp_ref[b, :, _lanes(s)]
            sc.ext_c[b * SLABS + s, CONV_PAD:, :] = _slab(u_conv, s, rows)
            sc.ext_p[b * SLABS + s, POOL_PAD:, :] = _slab(u_pool, s, rows)

    o_ref[...] = _mix_tail(h, w, sc, n_streams, n_rows)
    for s in range(SLABS):
        for b in range(n_streams):
            nc_ref[b, :, _lanes(s)] = sc.ext_c[b * SLABS + s, n_rows + CONV_OFF:, :]
            np_ref[b, :, _lanes(s)] = sc.ext_p[b * SLABS + s, n_rows + POOL_OFF:, :]


def _const_spec(shape):
    zeros = (0,) * len(shape)
    return pl.BlockSpec(shape, lambda *_: zeros, pipeline_mode=pl.Buffered(1))


def _mix_prompt(h, hist_c, hist_p, weights):
    t, d = h.shape
    tm = MIX_ROWS
    assert t % tm == 0
    w_args, w_specs = _mix_weight_operands(weights)
    return pl.pallas_call(
        _mix_prompt_kernel,
        out_shape=(jax.ShapeDtypeStruct((t, d), jnp.float32),
                   jax.ShapeDtypeStruct((SLABS, CONV_PAD, LANES), jnp.float32),
                   jax.ShapeDtypeStruct((SLABS, POOL_PAD, LANES), jnp.float32)),
        grid=(t // tm,),
        in_specs=[pl.BlockSpec((tm, d), lambda i: (i, 0)),
                  _const_spec((SLABS, CONV_PAD, LANES)),
                  _const_spec((SLABS, POOL_PAD, LANES))] + w_specs,
        out_specs=(pl.BlockSpec((tm, d), lambda i: (i, 0)),
                   pl.BlockSpec((SLABS, CONV_PAD, LANES), lambda i: (0, 0, 0)),
                   pl.BlockSpec((SLABS, POOL_PAD, LANES), lambda i: (0, 0, 0))),
        scratch_shapes=_mix_scratch_shapes(1, tm),
        compiler_params=pltpu.CompilerParams(
            dimension_semantics=("arbitrary",),
            vmem_limit_bytes=_mix_vmem_limit(
                w_args, w_specs,
                [(SLABS, CONV_PAD, LANES), (SLABS, POOL_PAD, LANES)],
                [(tm, d), (tm, d), (SLABS, CONV_PAD, LANES), (SLABS, POOL_PAD, LANES)],
                _mix_scratch_shapes(1, tm))),
        name="mix_prompt",
    )(h, hist_c, hist_p, *w_args)


def _mix_small(h, state_c, state_p, weights):
    t, d = h.shape
    n_streams, _, _ = state_c.shape
    n_tok = t - N_META
    n_rows = n_tok // n_streams
    grp = MIX_SMALL_STREAMS
    tm = grp * n_rows
    assert n_rows * n_streams == n_tok and n_rows % SUBLANES == 0
    assert n_streams % grp == 0 and n_tok % N_META == 0 and tm >= N_META
    w_args, w_specs = _mix_weight_operands(weights)
    return pl.pallas_call(
        _mix_small_kernel,
        out_shape=(jax.ShapeDtypeStruct((n_tok, d), jnp.float32),
                   jax.ShapeDtypeStruct((n_streams, CONV_HIST, CONV_CH), jnp.float32),
                   jax.ShapeDtypeStruct((n_streams, POOL_HIST, POOL_CH), jnp.float32),
                   jax.ShapeDtypeStruct((SLABS, N_META, LANES), jnp.float32),
                   jax.ShapeDtypeStruct((SLABS, N_META, LANES), jnp.float32)),
        grid=(n_streams // grp,),
        in_specs=[pl.BlockSpec((tm, d), lambda i: (i, 0)),
                  pl.BlockSpec((N_META, d), lambda i: (n_tok // N_META, 0), pipeline_mode=pl.Buffered(1)),
                  pl.BlockSpec((grp, CONV_HIST, CONV_CH), lambda i: (i, 0, 0)),
                  pl.BlockSpec((grp, POOL_HIST, POOL_CH), lambda i: (i, 0, 0))]
        + w_specs,
        out_specs=(pl.BlockSpec((tm, d), lambda i: (i, 0)),
                   pl.BlockSpec((grp, CONV_HIST, CONV_CH), lambda i: (i, 0, 0)),
                   pl.BlockSpec((grp, POOL_HIST, POOL_CH), lambda i: (i, 0, 0)),
                   pl.BlockSpec((SLABS, N_META, LANES), lambda i: (0, 0, 0)),
                   pl.BlockSpec((SLABS, N_META, LANES), lambda i: (0, 0, 0))),
        scratch_shapes=_mix_scratch_shapes(grp, n_rows),
        compiler_params=pltpu.CompilerParams(
            dimension_semantics=("arbitrary",),
            vmem_limit_bytes=_mix_vmem_limit(
                w_args, w_specs,
                [(N_META, d)],
                [(tm, d), (tm, d)] + 2 * [(grp, CONV_HIST, CONV_CH), (grp, POOL_HIST, POOL_CH),
                                          (SLABS, N_META, LANES)],
                _mix_scratch_shapes(grp, n_rows))),
        name="mix_small",
    )(h, h, state_c, state_p, *w_args)


def _to_slabs(v):
    return v.reshape(v.shape[0], SLABS, LANES).transpose(1, 0, 2)


def _from_slabs(v):
    return v.transpose(1, 0, 2).reshape(v.shape[1], SLABS * LANES)


def kernel(x_prompt, x_sample, state_conv, state_pool, meta_tokens, ffn1_norm, ffn1_w_gate, ffn1_w_up, ffn1_w_down, mix_norm, w_in, conv_w, conv_b, conv_norm, pool_w, pool_scale, w_out, ffn2_norm, ffn2_w_gate, ffn2_w_up, ffn2_w_down, final_norm):
    depth = ffn1_norm.shape[0]
    b_p, seq, d = x_prompt.shape
    b_s, seq_s, _ = x_sample.shape
    assert depth == 1 and b_p == 1
    bf = jnp.bfloat16

    xp = x_prompt.reshape(seq, d)
    xs = jnp.concatenate([x_sample.reshape(b_s * seq_s, d), meta_tokens.astype(x_sample.dtype)], axis=0)

    row = lambda v: v.reshape(1, -1)
    fg = row(final_norm)
    hs, wg1, wu1, wd1 = _ffn(xs, row(ffn1_norm[0]), ffn1_w_gate[0], ffn1_w_up[0], ffn1_w_down[0], fg,
                             final_norm=False, name="ffn1_small")
    hp, wg2, wu2, wd2, w_in_bf, w_out_bf = _ffn(
        xp, row(ffn1_norm[0]), wg1, wu1, wd1, fg,
        cast=(ffn2_w_gate[0], ffn2_w_up[0], ffn2_w_down[0], w_in[0], w_out[0]),
        final_norm=False, name="ffn1_prompt")
    ffn2 = (row(ffn2_norm[0]), wg2, wu2, wd2, fg)
    mixw = _MixWeights(
        norm=row(mix_norm[0]),
        w_in=w_in_bf,
        conv_w=jnp.broadcast_to(_to_slabs(conv_w[0])[:, :, None, :], (SLABS, CONV_K, SUBLANES, LANES)),
        conv_b=jnp.broadcast_to(_to_slabs(row(conv_b[0])), (SLABS, SUBLANES, LANES)),
        conv_norm=row(conv_norm[0]),
        pool_w=pool_w[0].astype(bf),
        pool_scale=row(pool_scale[0]),
        w_out=w_out_bf)

    hs, new_conv_s, new_pool_s, meta_c, meta_p = _mix_small(hs, state_conv[0], state_pool[0], mixw)
    hist_c = jnp.concatenate([jnp.zeros((SLABS, CONV_PAD - N_META, LANES), jnp.float32), meta_c], axis=1)
    hp, conv_tail, pool_tail = _mix_prompt(hp, hist_c, meta_p, mixw)

    ys = _ffn(hs, *ffn2, final_norm=True, name="ffn2_small")
    yp = _ffn(hp, *ffn2, final_norm=True, name="ffn2_prompt")

    y_prompt = yp.reshape(b_p, seq, d)
    y_sample = ys.reshape(b_s, seq_s, d)
    new_conv_prompt = _from_slabs(conv_tail[:, CONV_OFF:, :])[None, None]
    new_pool_prompt = _from_slabs(pool_tail[:, POOL_OFF:, :])[None, None]
    return (y_prompt, y_sample, new_conv_prompt, new_pool_prompt, new_conv_s[None], new_pool_s[None])
```
